```python
import math
import jax
import jax.numpy as jnp
from jax import lax
import numpy as np

D_MODEL = 2048
BATCH = 8
SEQ = 2048
DEPTH = 2

EPS = 1e-6
Q_BLOCK = 128
REL_BUCKETS = 32
REL_MAX_DIST = 128
REL_HEADS = 8

A_HEADS = 8
A_DH = 64
A_DV = 2 * A_DH

B_HEADS = 8
B_DK = 64
B_DV = 128
RET_CHUNK = 128
ROPE_BASE = 10000.0

C_DINNER = D_MODEL
C_HEADDIM = 64
C_HEADS = C_DINNER // C_HEADDIM
C_GROUPS = 4
C_DSTATE = 128
C_CONV = 4
C_CONV_CH = C_DINNER + 2 * C_GROUPS * C_DSTATE
SSD_CHUNK = 128

D_HEADS = 8
D_DH = 128
D_LATENT = 256
IDX_HEADS = 16
IDX_DIM = 64
IDX_TOPK_MAX = 256

PEER_HEADS = 8
PEER_NKEYS = 128
PEER_NEXPERTS = PEER_NKEYS * PEER_NKEYS
PEER_DQ = 256
PEER_TOPK = 16
PEER_CHUNK = 128

AB_WIDTHS = (A_HEADS * 2 * A_DH, A_HEADS * 2 * A_DH, A_HEADS * A_DV,
             B_HEADS * B_DK, B_HEADS * B_DK, B_HEADS * B_DV, B_HEADS * B_DV)
AB_IN = sum(AB_WIDTHS)
AB_MIX = A_HEADS * A_DV + B_HEADS * B_DV
CD_WIDTHS = (C_DINNER, C_CONV_CH, C_HEADS,
             D_HEADS * D_DH, D_LATENT, IDX_HEADS * IDX_DIM, IDX_DIM, IDX_HEADS)
CD_IN = sum(CD_WIDTHS)
CD_MIX = C_DINNER + D_HEADS * D_DH

kernel_name = 'hybrid_diffret_ssd_dsa_peer'


def rms_f32(x, g):
    xf = x.astype(jnp.float32)
    return xf * lax.rsqrt(jnp.mean(xf * xf, axis=-1, keepdims=True) + EPS) * g.astype(jnp.float32)


def rmsnorm(x, g):
    return rms_f32(x, g).astype(x.dtype)


def split_cols(a, widths):
    offs = np.cumsum(widths)[:-1].tolist()
    return jnp.split(a, offs, axis=-1)


def to_blocks(a, size):
    return a.reshape(a.shape[0], a.shape[1] // size, size, *a.shape[2:]).swapaxes(0, 1)


def from_blocks(a):
    a = a.swapaxes(0, 1)
    return a.reshape(a.shape[0], a.shape[1] * a.shape[2], *a.shape[3:])


def rel_bucket(dist):
    n = jnp.maximum(dist, 0)
    max_exact = REL_BUCKETS // 2
    nf = jnp.maximum(n, 1).astype(jnp.float32)
    large = max_exact + (jnp.log(nf / max_exact) / math.log(REL_MAX_DIST / max_exact)
                         * (REL_BUCKETS - max_exact)).astype(jnp.int32)
    large = jnp.minimum(large, REL_BUCKETS - 1)
    return jnp.where(n < max_exact, n, large)


def rope(x, pos):
    d = x.shape[-1]
    inv = ROPE_BASE ** (-jnp.arange(0, d, 2, dtype=jnp.float32) / d)
    ang = pos.astype(jnp.float32)[:, None] * inv[None, :]
    cos = jnp.cos(ang)[None, :, None, :]
    sin = jnp.sin(ang)[None, :, None, :]
    xf = x.astype(jnp.float32)
    x1, x2 = xf[..., 0::2], xf[..., 1::2]
    return jnp.stack([x1 * cos - x2 * sin, x1 * sin + x2 * cos], axis=-1).reshape(x.shape).astype(x.dtype)


def diff_attention(q, k, v, lam, rel_table):
    bsz, s_len = q.shape[0], q.shape[1]
    nb = s_len // Q_BLOCK
    kpos = jnp.arange(s_len)
    scale = A_DH ** -0.5

    def block(args):
        qblk, i = args
        qpos = i * Q_BLOCK + jnp.arange(Q_BLOCK)
        dist = qpos[:, None] - kpos[None, :]
        bias = rel_table[rel_bucket(dist)].astype(jnp.float32).transpose(2, 0, 1)
        bias = jnp.where((dist >= 0)[None], bias, -jnp.inf)
        logits = jnp.einsum('bqhmd,bshmd->bmhqs', qblk, k).astype(jnp.float32) * scale + bias[None, None]
        p = jax.nn.softmax(logits, axis=-1)
        w = p[:, 0] - lam * p[:, 1]
        return jnp.einsum('bhqs,bshd->bqhd', w.astype(v.dtype), v)

    out = lax.map(block, (to_blocks(q, Q_BLOCK), jnp.arange(nb)))
    return from_blocks(out)


def retention(q, k, v):
    bsz, s_len, n_h, dk = q.shape
    dv = v.shape[-1]
    c = RET_CHUNK
    log_gamma = jnp.log(1.0 - 2.0 ** (-5.0 - jnp.arange(n_h, dtype=jnp.float32)))
    idx = jnp.arange(c, dtype=jnp.float32)
    rel = idx[:, None] - idx[None, :]
    inner_decay = jnp.where(rel[None] >= 0, jnp.exp(rel[None] * log_gamma[:, None, None]), 0.0)
    q_decay = jnp.exp((idx[:, None] + 1.0) * log_gamma[None, :])
    k_decay = jnp.exp((c - 1.0 - idx[:, None]) * log_gamma[None, :])
    chunk_decay = jnp.exp(c * log_gamma)
    qc = to_blocks(q.astype(jnp.float32) * dk ** -0.5, c)
    kc = to_blocks(k.astype(jnp.float32), c)
    vc = to_blocks(v.astype(jnp.float32), c)

    def step(state, inp):
        qi, ki, vi = inp
        scores = jnp.einsum('bqhd,bshd->bhqs', qi, ki) * inner_decay[None]
        o = jnp.einsum('bhqs,bshv->bqhv', scores, vi)
        o = o + jnp.einsum('bqhd,bhdv->bqhv', qi * q_decay[None, :, :, None], state)
        state = state * chunk_decay[None, :, None, None] + jnp.einsum(
            'bshd,bshv->bhdv', ki * k_decay[None, :, :, None], vi)
        return state, o

    _, o = lax.scan(step, jnp.zeros((bsz, n_h, dk, dv), jnp.float32), (qc, kc, vc))
    return from_blocks(o)


def causal_dwconv(x, w, b):
    k_w = w.shape[0]
    out = lax.conv_general_dilated(x, w[:, None, :], window_strides=(1,), padding=[(k_w - 1, 0)],
                                   dimension_numbers=('NWC', 'WIO', 'NWC'),
                                   feature_group_count=x.shape[-1])
    return out + b


def ssd_scan(x, dt, a, bm, cm):
    bsz, s_len, n_h, p = x.shape
    g, n = bm.shape[2], bm.shape[3]
    hg = n_h // g
    qn = SSD_CHUNK
    f32 = jnp.float32
    xs = to_blocks(x.astype(f32).reshape(bsz, s_len, g, hg, p), qn)
    dts = to_blocks(dt.reshape(bsz, s_len, g, hg), qn)
    bs = to_blocks(bm.astype(f32), qn)
    cs_in = to_blocks(cm.astype(f32), qn)
    ag = a.reshape(g, hg)
    causal = jnp.tril(jnp.ones((qn, qn), bool))[None, :, :, None, None]

    def step(state, inp):
        xc, dtc, bc, cc = inp
        cs = jnp.cumsum(dtc * ag, axis=1)
        seg = cs[:, :, None] - cs[:, None, :]
        lmat = jnp.exp(jnp.where(causal, seg, -jnp.inf))
        cb = jnp.einsum('blgn,bsgn->blsg', cc, bc)
        xdt = xc * dtc[..., None]
        y = jnp.einsum('blsg,blsgh,bsghp->blghp', cb, lmat, xdt)
        y = y + jnp.einsum('blgn,bghpn->blghp', cc, state) * jnp.exp(cs)[..., None]
        last = cs[:, -1]
        to_end = jnp.exp(last[:, None] - cs)
        state = state * jnp.exp(last)[..., None, None] + jnp.einsum(
            'bsgn,bsghp->bghpn', bc, xdt * to_end[..., None])
        return state, y

    _, y = lax.scan(step, jnp.zeros((bsz, g, hg, p, n), f32), (xs, dts, bs, cs_in))
    return from_blocks(y).reshape(bsz, s_len, n_h, p)


def mamba2_mixer(z, xbc, dt_raw, conv_w, conv_b, dt_bias, a_log, d_skip, norm_g):
    bsz, s_len = z.shape[0], z.shape[1]
    xbc = jax.nn.silu(causal_dwconv(xbc, conv_w, conv_b))
    xs, bm, cm = split_cols(xbc, (C_DINNER, C_GROUPS * C_DSTATE, C_GROUPS * C_DSTATE))
    dt = jax.nn.softplus(dt_raw.astype(jnp.float32) + dt_bias.astype(jnp.float32))
    a = -jnp.exp(a_log.astype(jnp.float32))
    xh = xs.reshape(bsz, s_len, C_HEADS, C_HEADDIM)
    y = ssd_scan(xh, dt, a,
                 bm.reshape(bsz, s_len, C_GROUPS, C_DSTATE),
                 cm.reshape(bsz, s_len, C_GROUPS, C_DSTATE))
    y = y + xh.astype(jnp.float32) * d_skip.astype(jnp.float32)[:, None]
    y = y.reshape(bsz, s_len, C_DINNER) * jax.nn.silu(z.astype(jnp.float32))
    yg = y.reshape(bsz, s_len, C_GROUPS, C_DINNER // C_GROUPS)
    yg = yg * lax.rsqrt(jnp.mean(yg * yg, axis=-1, keepdims=True) + EPS)
    return (yg.reshape(bsz, s_len, C_DINNER) * norm_g.astype(jnp.float32)).astype(z.dtype)


def dsa_mixer(q, ckv, q_idx, k_idx, w_idx, kv_norm, w_uk, w_uv, rel_table):
    bsz, s_len = q.shape[0], q.shape[1]
    topk = min(IDX_TOPK_MAX, s_len // 4)
    nb = s_len // Q_BLOCK
    ckv = rmsnorm(ckv, kv_norm)
    q_abs = jnp.einsum('bshd,hdc->bshc', q, w_uk)
    kpos = jnp.arange(s_len)
    bidx = jnp.arange(bsz)[:, None, None]
    idx_scale = (IDX_HEADS * IDX_DIM) ** -0.5

    def block(args):
        qa, qi, wi, i = args
        qpos = i * Q_BLOCK + jnp.arange(Q_BLOCK)
        causal = qpos[:, None] >= kpos[None, :]
        rel = jax.nn.relu(jnp.einsum('bqhd,bsd->bqhs', qi, k_idx).astype(jnp.float32))
        score = jnp.einsum('bqh,bqhs->bqs', wi.astype(jnp.float32) * idx_scale, rel)
        score = jnp.where(causal[None], score, -jnp.inf)
        _, sel = lax.top_k(score, topk)
        kv_sel = ckv[bidx, sel]
        dist = qpos[None, :, None] - sel
        valid = dist >= 0
        bias = rel_table[rel_bucket(dist)].astype(jnp.float32)
        logits = jnp.einsum('bqhc,bqkc->bqhk', qa, kv_sel).astype(jnp.float32) * D_DH ** -0.5
        logits = logits + bias.transpose(0, 1, 3, 2)
        logits = jnp.where(valid[:, :, None, :], logits, -jnp.inf)
        p = jax.nn.softmax(logits, axis=-1)
        ctx = jnp.einsum('bqhk,bqkc->bqhc', p.astype(kv_sel.dtype), kv_sel)
        return jnp.einsum('bqhc,hcd->bqhd', ctx, w_uv)

    out = lax.map(block, (to_blocks(q_abs, Q_BLOCK), to_blocks(q_idx, Q_BLOCK),
                          to_blocks(w_idx, Q_BLOCK), jnp.arange(nb)))
    return from_blocks(out)


def ab_mixer(h, w_in, w_out, lam_p, a_norm, b_norm, rel_table, layer):
    bsz, s_len, _ = h.shape
    qa, ka, va, qb, kb, vb, gb = split_cols(h @ w_in, AB_WIDTHS)
    lam_init = 0.8 - 0.6 * math.exp(-0.3 * layer)
    lp = lam_p.astype(jnp.float32)
    lam = jnp.exp(jnp.sum(lp[0] * lp[1])) - jnp.exp(jnp.sum(lp[2] * lp[3])) + lam_init
    oa = diff_attention(qa.reshape(bsz, s_len, A_HEADS, 2, A_DH),
                        ka.reshape(bsz, s_len, A_HEADS, 2, A_DH),
                        va.reshape(bsz, s_len, A_HEADS, A_DV), lam, rel_table)
    oa = rms_f32(oa, a_norm) * (1.0 - lam_init)
    pos = jnp.arange(s_len)
    ob = retention(rope(qb.reshape(bsz, s_len, B_HEADS, B_DK), pos),
                   rope(kb.reshape(bsz, s_len, B_HEADS, B_DK), pos),
                   vb.reshape(bsz, s_len, B_HEADS, B_DV))
    ob = rms_f32(ob, b_norm) * jax.nn.silu(gb.astype(jnp.float32)).reshape(bsz, s_len, B_HEADS, B_DV)
    mixed = jnp.concatenate([oa.reshape(bsz, s_len, -1), ob.reshape(bsz, s_len, -1)], axis=-1)
    return mixed.astype(h.dtype) @ w_out


def cd_mixer(h, w_in, w_out, conv_w, conv_b, dt_bias, a_log, d_skip, ssm_norm,
             kv_norm, w_uk, w_uv, rel_table):
    bsz, s_len, _ = h.shape
    z, xbc, dt, q, ckv, q_idx, k_idx, w_idx = split_cols(h @ w_in, CD_WIDTHS)
    yc = mamba2_mixer(z, xbc, dt, conv_w, conv_b, dt_bias, a_log, d_skip, ssm_norm)
    yd = dsa_mixer(q.reshape(bsz, s_len, D_HEADS, D_DH), ckv,
                   q_idx.reshape(bsz, s_len, IDX_HEADS, IDX_DIM), k_idx, w_idx,
                   kv_norm, w_uk, w_uv, rel_table)
    mixed = jnp.concatenate([yc, yd.reshape(bsz, s_len, -1).astype(yc.dtype)], axis=-1)
    return mixed.astype(h.dtype) @ w_out


def peer(h, w_q, sub_keys, u, v):
    bsz, s_len, d = h.shape
    t = bsz * s_len
    ht = h.reshape(t, d)
    q = (ht @ w_q).reshape(t, PEER_HEADS, 2, PEER_DQ // 2)
    s = jnp.einsum('thcd,hckd->thck', q, sub_keys).astype(jnp.float32)
    top_s, top_i = lax.top_k(s, PEER_TOPK)
    cand = (top_s[:, :, 0, :, None] + top_s[:, :, 1, None, :]).reshape(t, PEER_HEADS, -1)
    cand_id = (top_i[:, :, 0, :, None] * PEER_NKEYS + top_i[:, :, 1, None, :]).reshape(t, PEER_HEADS, -1)
    best, pos = lax.top_k(cand, PEER_TOPK)
    ids = jnp.take_along_axis(cand_id, pos, axis=-1)
    gate = jax.nn.softmax(best, axis=-1)
    nch = t // PEER_CHUNK

    def chunk(args):
        xc, idc, gc = args
        act = jax.nn.gelu(jnp.einsum('td,thkd->thk', xc, u[idc]).astype(jnp.float32))
        return jnp.einsum('thk,thkd->td', (gc * act).astype(v.dtype), v[idc])

    out = lax.map(chunk, (ht.reshape(nch, PEER_CHUNK, d),
                          ids.reshape(nch, PEER_CHUNK, PEER_HEADS, PEER_TOPK),
                          gate.reshape(nch, PEER_CHUNK, PEER_HEADS, PEER_TOPK)))
    return out.reshape(bsz, s_len, d).astype(h.dtype)


def setup_inputs(seed: int = 0) -> dict:
    key = jax.random.key(seed)
    keys = list(jax.random.split(key, 32))
    f32 = jnp.float32
    ne = (DEPTH + 1) // 2
    no = DEPTH // 2

    def nrm(shape, scale):
        return jax.random.normal(keys.pop(), shape, f32) * scale

    def gain(shape):
        return 1.0 + nrm(shape, 0.02)

    dt = jnp.exp(jax.random.uniform(keys.pop(), (no, C_HEADS), f32, math.log(1e-3), math.log(1e-1)))
    a_init = jax.random.uniform(keys.pop(), (no, C_HEADS), f32, 1.0, 16.0)
    return {
        'x': nrm((BATCH, SEQ, D_MODEL), 1.0),
        'rel_table': nrm((REL_BUCKETS, REL_HEADS), 0.2),
        'ab_w_in': nrm((ne, D_MODEL, AB_IN), D_MODEL ** -0.5),
        'ab_w_out': nrm((ne, AB_MIX, D_MODEL), AB_MIX ** -0.5),
        'ab_lambda': nrm((ne, 4, A_DH), 0.1),
        'ab_a_norm': gain((ne, A_DV)),
        'ab_b_norm': gain((ne, B_DV)),
        'cd_w_in': nrm((no, D_MODEL, CD_IN), D_MODEL ** -0.5),
        'cd_w_out': nrm((no, CD_MIX, D_MODEL), CD_MIX ** -0.5),
        'cd_conv_w': nrm((no, C_CONV, C_CONV_CH), C_CONV ** -0.5),
        'cd_conv_b': nrm((no, C_CONV_CH), 0.02),
        'cd_dt_bias': dt + jnp.log(-jnp.expm1(-dt)),
        'cd_a_log': jnp.log(a_init),
        'cd_d_skip': gain((no, C_HEADS)),
        'cd_ssm_norm': gain((no, C_DINNER)),
        'cd_kv_norm': gain((no, D_LATENT)),
        'cd_w_uk': nrm((no, D_HEADS, D_DH, D_LATENT), D_DH ** -0.5),
        'cd_w_uv': nrm((no, D_HEADS, D_LATENT, D_DH), D_LATENT ** -0.5),
        'norm_mix': gain((DEPTH, D_MODEL)),
        'norm_ffn': gain((DEPTH, D_MODEL)),
        'peer_w_q': nrm((DEPTH, D_MODEL, PEER_HEADS * PEER_DQ), D_MODEL ** -0.5),
        'peer_keys': nrm((DEPTH, PEER_HEADS, 2, PEER_NKEYS, PEER_DQ // 2), (PEER_DQ // 2) ** -0.5),
        'peer_u': nrm((DEPTH, PEER_NEXPERTS, D_MODEL), D_MODEL ** -0.5),
        'peer_v': nrm((DEPTH, PEER_NEXPERTS, D_MODEL), (PEER_HEADS * PEER_TOPK) ** -0.5),
        'norm_final': gain((D_MODEL,)),
    }


def reference(x, rel_table, ab_w_in, ab_w_out, ab_lambda, ab_a_norm, ab_b_norm,
              cd_w_in, cd_w_out, cd_conv_w, cd_conv_b, cd_dt_bias, cd_a_log, cd_d_skip,
              cd_ssm_norm, cd_kv_norm, cd_w_uk, cd_w_uv, norm_mix, norm_ffn,
              peer_w_q, peer_keys, peer_u, peer_v, norm_final):
    h = x
    for layer in range(DEPTH):
        hn = rmsnorm(h, norm_mix[layer])
        i = layer // 2
        if layer % 2 == 0:
            mix = ab_mixer(hn, ab_w_in[i], ab_w_out[i], ab_lambda[i], ab_a_norm[i], ab_b_norm[i],
                           rel_table, layer)
        else:
            mix = cd_mixer(hn, cd_w_in[i], cd_w_out[i], cd_conv_w[i], cd_conv_b[i], cd_dt_bias[i],
                           cd_a_log[i], cd_d_skip[i], cd_ssm_norm[i], cd_kv_norm[i],
                           cd_w_uk[i], cd_w_uv[i], rel_table)
        h = h + mix.astype(h.dtype)
        h = h + peer(rmsnorm(h, norm_ffn[layer]), peer_w_q[layer], peer_keys[layer],
                     peer_u[layer], peer_v[layer])
    return rmsnorm(h, norm_final)
```

```python
import functools
import math

import jax
import jax.numpy as jnp
import numpy as np
from jax import lax
from jax.experimental import pallas as pl
from jax.experimental.pallas import tpu as pltpu

D_MODEL = 2048
DEPTH = 2
EPS = 1e-6
Q_BLOCK = 128
REL_BUCKETS = 32
REL_MAX_DIST = 128
A_HEADS = 8
A_DH = 64
A_DV = 2 * A_DH
B_HEADS = 8
B_DK = 64
B_DV = 128
RET_CHUNK = 128
ROPE_BASE = 10000.0
C_DINNER = D_MODEL
C_HEADDIM = 64
C_HEADS = C_DINNER // C_HEADDIM
C_GROUPS = 4
C_DSTATE = 128
C_CONV = 4
C_CONV_CH = C_DINNER + 2 * C_GROUPS * C_DSTATE
SSD_CHUNK = 128
D_HEADS = 8
D_DH = 128
D_LATENT = 256
IDX_HEADS = 16
IDX_DIM = 64
IDX_TOPK_MAX = 256
PEER_HEADS = 8
PEER_NKEYS = 128
PEER_DQ = 256
PEER_TOPK = 16
PEER_CHUNK = 128
AB_WIDTHS = (A_HEADS * 2 * A_DH, A_HEADS * 2 * A_DH, A_HEADS * A_DV,
             B_HEADS * B_DK, B_HEADS * B_DK, B_HEADS * B_DV, B_HEADS * B_DV)
CD_WIDTHS = (C_DINNER, C_CONV_CH, C_HEADS,
             D_HEADS * D_DH, D_LATENT, IDX_HEADS * IDX_DIM, IDX_DIM, IDX_HEADS)


def rms_f32(x, g):
    xf = x.astype(jnp.float32)
    return xf * lax.rsqrt(jnp.mean(xf * xf, axis=-1, keepdims=True) + EPS) * g.astype(jnp.float32)


def rmsnorm(x, g):
    return rms_f32(x, g).astype(x.dtype)


def split_cols(a, widths):
    offs = np.cumsum(widths)[:-1].tolist()
    return jnp.split(a, offs, axis=-1)


def to_blocks(a, size):
    return a.reshape(a.shape[0], a.shape[1] // size, size, *a.shape[2:]).swapaxes(0, 1)


def from_blocks(a):
    a = a.swapaxes(0, 1)
    return a.reshape(a.shape[0], a.shape[1] * a.shape[2], *a.shape[3:])


def rel_bucket(dist):
    n = jnp.maximum(dist, 0)
    max_exact = REL_BUCKETS // 2
    nf = jnp.maximum(n, 1).astype(jnp.float32)
    large = max_exact + (jnp.log(nf / max_exact) / math.log(REL_MAX_DIST / max_exact)
                         * (REL_BUCKETS - max_exact)).astype(jnp.int32)
    large = jnp.minimum(large, REL_BUCKETS - 1)
    return jnp.where(n < max_exact, n, large)


def rope(x, pos):
    d = x.shape[-1]
    inv = ROPE_BASE ** (-jnp.arange(0, d, 2, dtype=jnp.float32) / d)
    ang = pos.astype(jnp.float32)[:, None] * inv[None, :]
    cos = jnp.cos(ang)[None, :, None, :]
    sin = jnp.sin(ang)[None, :, None, :]
    xf = x.astype(jnp.float32)
    x1, x2 = xf[..., 0::2], xf[..., 1::2]
    return jnp.stack([x1 * cos - x2 * sin, x1 * sin + x2 * cos], axis=-1).reshape(x.shape).astype(x.dtype)


def diff_attention(q, k, v, lam, rel_table):
    bsz, s_len = q.shape[0], q.shape[1]
    nb = s_len // Q_BLOCK
    kpos = jnp.arange(s_len)
    scale = A_DH ** -0.5

    def block(args):
        qblk, i = args
        qpos = i * Q_BLOCK + jnp.arange(Q_BLOCK)
        dist = qpos[:, None] - kpos[None, :]
        bias = rel_table[rel_bucket(dist)].astype(jnp.float32).transpose(2, 0, 1)
        bias = jnp.where((dist >= 0)[None], bias, -jnp.inf)
        logits = jnp.einsum('bqhmd,bshmd->bmhqs', qblk, k).astype(jnp.float32) * scale + bias[None, None]
        p = jax.nn.softmax(logits, axis=-1)
        w = p[:, 0] - lam * p[:, 1]
        return jnp.einsum('bhqs,bshd->bqhd', w.astype(v.dtype), v)

    out = lax.map(block, (to_blocks(q, Q_BLOCK), jnp.arange(nb)))
    return from_blocks(out)


def retention(q, k, v):
    bsz, s_len, n_h, dk = q.shape
    c = RET_CHUNK
    log_gamma = jnp.log(1.0 - 2.0 ** (-5.0 - jnp.arange(n_h, dtype=jnp.float32)))
    idx = jnp.arange(c, dtype=jnp.float32)
    rel = idx[:, None] - idx[None, :]
    inner_decay = jnp.where(rel[None] >= 0, jnp.exp(rel[None] * log_gamma[:, None, None]), 0.0)
    q_decay = jnp.exp((idx[:, None] + 1.0) * log_gamma[None, :])
    k_decay = jnp.exp((c - 1.0 - idx[:, None]) * log_gamma[None, :])
    chunk_decay = jnp.exp(c * log_gamma)
    qc = to_blocks(q.astype(jnp.float32) * dk ** -0.5, c)
    kc = to_blocks(k.astype(jnp.float32), c)
    vc = to_blocks(v.astype(jnp.float32), c)
    dv = v.shape[-1]

    def step(state, inp):
        qi, ki, vi = inp
        scores = jnp.einsum('bqhd,bshd->bhqs', qi, ki) * inner_decay[None]
        o = jnp.einsum('bhqs,bshv->bqhv', scores, vi)
        o = o + jnp.einsum('bqhd,bhdv->bqhv', qi * q_decay[None, :, :, None], state)
        state = state * chunk_decay[None, :, None, None] + jnp.einsum(
            'bshd,bshv->bhdv', ki * k_decay[None, :, :, None], vi)
        return state, o

    _, o = lax.scan(step, jnp.zeros((bsz, n_h, dk, dv), jnp.float32), (qc, kc, vc))
    return from_blocks(o)


def causal_dwconv(x, w, b):
    k_w = w.shape[0]
    out = lax.conv_general_dilated(x, w[:, None, :], window_strides=(1,), padding=[(k_w - 1, 0)],
                                   dimension_numbers=('NWC', 'WIO', 'NWC'),
                                   feature_group_count=x.shape[-1])
    return out + b


def ssd_scan(x, dt, a, bm, cm):
    bsz, s_len, n_h, p = x.shape
    g, n = bm.shape[2], bm.shape[3]
    hg = n_h // g
    qn = SSD_CHUNK
    f32 = jnp.float32
    xs = to_blocks(x.astype(f32).reshape(bsz, s_len, g, hg, p), qn)
    dts = to_blocks(dt.reshape(bsz, s_len, g, hg), qn)
    bs = to_blocks(bm.astype(f32), qn)
    cs_in = to_blocks(cm.astype(f32), qn)
    ag = a.reshape(g, hg)
    causal = jnp.tril(jnp.ones((qn, qn), bool))[None, :, :, None, None]

    def step(state, inp):
        xc, dtc, bc, cc = inp
        cs = jnp.cumsum(dtc * ag, axis=1)
        seg = cs[:, :, None] - cs[:, None, :]
        lmat = jnp.exp(jnp.where(causal, seg, -jnp.inf))
        cb = jnp.einsum('blgn,bsgn->blsg', cc, bc)
        xdt = xc * dtc[..., None]
        y = jnp.einsum('blsg,blsgh,bsghp->blghp', cb, lmat, xdt)
        y = y + jnp.einsum('blgn,bghpn->blghp', cc, state) * jnp.exp(cs)[..., None]
        last = cs[:, -1]
        to_end = jnp.exp(last[:, None] - cs)
        state = state * jnp.exp(last)[..., None, None] + jnp.einsum(
            'bsgn,bsghp->bghpn', bc, xdt * to_end[..., None])
        return state, y

    _, y = lax.scan(step, jnp.zeros((bsz, g, hg, p, n), f32), (xs, dts, bs, cs_in))
    return from_blocks(y).reshape(bsz, s_len, n_h, p)


def mamba2_mixer(z, xbc, dt_raw, conv_w, conv_b, dt_bias, a_log, d_skip, norm_g):
    bsz, s_len = z.shape[0], z.shape[1]
    xbc = jax.nn.silu(causal_dwconv(xbc, conv_w, conv_b))
    xs, bm, cm = split_cols(xbc, (C_DINNER, C_GROUPS * C_DSTATE, C_GROUPS * C_DSTATE))
    dt = jax.nn.softplus(dt_raw.astype(jnp.float32) + dt_bias.astype(jnp.float32))
    a = -jnp.exp(a_log.astype(jnp.float32))
    xh = xs.reshape(bsz, s_len, C_HEADS, C_HEADDIM)
    y = ssd_scan(xh, dt, a,
                 bm.reshape(bsz, s_len, C_GROUPS, C_DSTATE),
                 cm.reshape(bsz, s_len, C_GROUPS, C_DSTATE))
    y = y + xh.astype(jnp.float32) * d_skip.astype(jnp.float32)[:, None]
    y = y.reshape(bsz, s_len, C_DINNER) * jax.nn.silu(z.astype(jnp.float32))
    yg = y.reshape(bsz, s_len, C_GROUPS, C_DINNER // C_GROUPS)
    yg = yg * lax.rsqrt(jnp.mean(yg * yg, axis=-1, keepdims=True) + EPS)
    return (yg.reshape(bsz, s_len, C_DINNER) * norm_g.astype(jnp.float32)).astype(z.dtype)


def dsa_mixer(q, ckv, q_idx, k_idx, w_idx, kv_norm, w_uk, w_uv, rel_table):
    bsz, s_len = q.shape[0], q.shape[1]
    topk = min(IDX_TOPK_MAX, s_len // 4)
    nb = s_len // Q_BLOCK
    ckv = rmsnorm(ckv, kv_norm)
    q_abs = jnp.einsum('bshd,hdc->bshc', q, w_uk)
    kpos = jnp.arange(s_len)
    bidx = jnp.arange(bsz)[:, None, None]
    idx_scale = (IDX_HEADS * IDX_DIM) ** -0.5

    def block(args):
        qa, qi, wi, i = args
        qpos = i * Q_BLOCK + jnp.arange(Q_BLOCK)
        causal = qpos[:, None] >= kpos[None, :]
        rel = jax.nn.relu(jnp.einsum('bqhd,bsd->bqhs', qi, k_idx).astype(jnp.float32))
        score = jnp.einsum('bqh,bqhs->bqs', wi.astype(jnp.float32) * idx_scale, rel)
        score = jnp.where(causal[None], score, -jnp.inf)
        _, sel = lax.top_k(score, topk)
        kv_sel = ckv[bidx, sel]
        dist = qpos[None, :, None] - sel
        valid = dist >= 0
        bias = rel_table[rel_bucket(dist)].astype(jnp.float32)
        logits = jnp.einsum('bqhc,bqkc->bqhk', qa, kv_sel).astype(jnp.float32) * D_DH ** -0.5
        logits = logits + bias.transpose(0, 1, 3, 2)
        logits = jnp.where(valid[:, :, None, :], logits, -jnp.inf)
        p = jax.nn.softmax(logits, axis=-1)
        ctx = jnp.einsum('bqhk,bqkc->bqhc', p.astype(kv_sel.dtype), kv_sel)
        return jnp.einsum('bqhc,hcd->bqhd', ctx, w_uv)

    out = lax.map(block, (to_blocks(q_abs, Q_BLOCK), to_blocks(q_idx, Q_BLOCK),
                          to_blocks(w_idx, Q_BLOCK), jnp.arange(nb)))
    return from_blocks(out)


def ab_mixer(h, w_in, w_out, lam_p, a_norm, b_norm, rel_table, layer):
    bsz, s_len, _ = h.shape
    qa, ka, va, qb, kb, vb, gb = split_cols(h @ w_in, AB_WIDTHS)
    lam_init = 0.8 - 0.6 * math.exp(-0.3 * layer)
    lp = lam_p.astype(jnp.float32)
    lam = jnp.exp(jnp.sum(lp[0] * lp[1])) - jnp.exp(jnp.sum(lp[2] * lp[3])) + lam_init
    oa = diff_attention(qa.reshape(bsz, s_len, A_HEADS, 2, A_DH),
                        ka.reshape(bsz, s_len, A_HEADS, 2, A_DH),
                        va.reshape(bsz, s_len, A_HEADS, A_DV), lam, rel_table)
    oa = rms_f32(oa, a_norm) * (1.0 - lam_init)
    pos = jnp.arange(s_len)
    ob = retention(rope(qb.reshape(bsz, s_len, B_HEADS, B_DK), pos),
                   rope(kb.reshape(bsz, s_len, B_HEADS, B_DK), pos),
                   vb.reshape(bsz, s_len, B_HEADS, B_DV))
    ob = rms_f32(ob, b_norm) * jax.nn.silu(gb.astype(jnp.float32)).reshape(bsz, s_len, B_HEADS, B_DV)
    mixed = jnp.concatenate([oa.reshape(bsz, s_len, -1), ob.reshape(bsz, s_len, -1)], axis=-1)
    return mixed.astype(h.dtype) @ w_out


def cd_mixer(h, w_in, w_out, conv_w, conv_b, dt_bias, a_log, d_skip, ssm_norm,
             kv_norm, w_uk, w_uv, rel_table):
    bsz, s_len, _ = h.shape
    z, xbc, dt, q, ckv, q_idx, k_idx, w_idx = split_cols(h @ w_in, CD_WIDTHS)
    yc = mamba2_mixer(z, xbc, dt, conv_w, conv_b, dt_bias, a_log, d_skip, ssm_norm)
    yd = dsa_mixer(q.reshape(bsz, s_len, D_HEADS, D_DH), ckv,
                   q_idx.reshape(bsz, s_len, IDX_HEADS, IDX_DIM), k_idx, w_idx,
                   kv_norm, w_uk, w_uv, rel_table)
    mixed = jnp.concatenate([yc, yd.reshape(bsz, s_len, -1).astype(yc.dtype)], axis=-1)
    return mixed.astype(h.dtype) @ w_out


def peer(h, w_q, sub_keys, u, v):
    bsz, s_len, d = h.shape
    t = bsz * s_len
    ht = h.reshape(t, d)
    q = (ht @ w_q).reshape(t, PEER_HEADS, 2, PEER_DQ // 2)
    s = jnp.einsum('thcd,hckd->thck', q, sub_keys).astype(jnp.float32)
    top_s, top_i = lax.top_k(s, PEER_TOPK)
    cand = (top_s[:, :, 0, :, None] + top_s[:, :, 1, None, :]).reshape(t, PEER_HEADS, -1)
    cand_id = (top_i[:, :, 0, :, None] * PEER_NKEYS + top_i[:, :, 1, None, :]).reshape(t, PEER_HEADS, -1)
    best, pos = lax.top_k(cand, PEER_TOPK)
    ids = jnp.take_along_axis(cand_id, pos, axis=-1)
    gate = jax.nn.softmax(best, axis=-1)
    nch = t // PEER_CHUNK

    def chunk(args):
        xc, idc, gc = args
        act = jax.nn.gelu(jnp.einsum('td,thkd->thk', xc, u[idc]).astype(jnp.float32))
        return jnp.einsum('thk,thkd->td', (gc * act).astype(v.dtype), v[idc])

    out = lax.map(chunk, (ht.reshape(nch, PEER_CHUNK, d),
                          ids.reshape(nch, PEER_CHUNK, PEER_HEADS, PEER_TOPK),
                          gate.reshape(nch, PEER_CHUNK, PEER_HEADS, PEER_TOPK)))
    return out.reshape(bsz, s_len, d).astype(h.dtype)


def _rmsnorm_kernel(x_ref, g_ref, o_ref):
    x = x_ref[...]
    ms = jnp.mean(x * x, axis=-1, keepdims=True)
    o_ref[...] = x * lax.rsqrt(ms + EPS) * g_ref[...]


def rmsnorm_pallas(x2d, g, tm=512):
    t, d = x2d.shape
    return pl.pallas_call(
        _rmsnorm_kernel,
        grid=(t // tm,),
        in_specs=[pl.BlockSpec((tm, d), lambda i: (i, 0)),
                  pl.BlockSpec((1, d), lambda i: (0, 0))],
        out_specs=pl.BlockSpec((tm, d), lambda i: (i, 0)),
        out_shape=jax.ShapeDtypeStruct((t, d), jnp.float32),
        name="rmsnorm_final",
    )(x2d, g.reshape(1, d))


def kernel(x, rel_table, ab_w_in, ab_w_out, ab_lambda, ab_a_norm, ab_b_norm, cd_w_in, cd_w_out, cd_conv_w, cd_conv_b, cd_dt_bias, cd_a_log, cd_d_skip, cd_ssm_norm, cd_kv_norm, cd_w_uk, cd_w_uv, norm_mix, norm_ffn, peer_w_q, peer_keys, peer_u, peer_v, norm_final):
    h = x
    for layer in range(DEPTH):
        hn = rmsnorm(h, norm_mix[layer])
        i = layer // 2
        if layer % 2 == 0:
            mix = ab_mixer(hn, ab_w_in[i], ab_w_out[i], ab_lambda[i], ab_a_norm[i], ab_b_norm[i],
                           rel_table, layer)
        else:
            mix = cd_mixer(hn, cd_w_in[i], cd_w_out[i], cd_conv_w[i], cd_conv_b[i], cd_dt_bias[i],
                           cd_a_log[i], cd_d_skip[i], cd_ssm_norm[i], cd_kv_norm[i],
                           cd_w_uk[i], cd_w_uv[i], rel_table)
        h = h + mix.astype(h.dtype)
        h = h + peer(rmsnorm(h, norm_ffn[layer]), peer_w_q[layer], peer_keys[layer],
                     peer_u[layer], peer_v[layer])
    b, s, d = h.shape
    return rmsnorm_pallas(h.reshape(b * s, d), norm_final).reshape(b, s, d)
```

```python
import functools
import math

import jax
import jax.numpy as jnp
import numpy as np
from jax import lax
from jax.experimental import pallas as pl
from jax.experimental.pallas import tpu as pltpu

D_MODEL = 2048
DEPTH = 2
EPS = 1e-6
Q_BLOCK = 128
REL_BUCKETS = 32
REL_MAX_DIST = 128
A_HEADS = 8
A_DH = 64
A_DV = 2 * A_DH
B_HEADS = 8
B_DK = 64
B_DV = 128
RET_CHUNK = 128
ROPE_BASE = 10000.0
C_DINNER = D_MODEL
C_HEADDIM = 64
C_HEADS = C_DINNER // C_HEADDIM
C_GROUPS = 4
C_HPG = C_HEADS // C_GROUPS
C_DSTATE = 128
C_CONV = 4
C_CONV_CH = C_DINNER + 2 * C_GROUPS * C_DSTATE
SSD_CHUNK = 128
D_HEADS = 8
D_DH = 128
D_LATENT = 256
IDX_HEADS = 16
IDX_DIM = 64
IDX_TOPK_MAX = 256
PEER_HEADS = 8
PEER_NKEYS = 128
PEER_DQ = 256
PEER_TOPK = 16

F32 = jnp.float32
BF16 = jnp.bfloat16
NEG = -1e30
LANES = 128
VMEM_LIMIT = 56 * 1024 * 1024
HIGHEST = lax.Precision.HIGHEST

CD_Z = 0
CD_XBC = CD_Z + C_DINNER
CD_Q = CD_XBC + C_CONV_CH
CD_QIDX = CD_Q + D_HEADS * D_DH
CD_CKV = CD_QIDX + IDX_HEADS * IDX_DIM
CD_KIDX = CD_CKV + D_LATENT
CD_MISC = CD_KIDX + LANES
CD_WIDTH = CD_MISC + LANES
MISC_WIDX = C_HEADS


def _dot(a, b):
    return jnp.dot(a, b, preferred_element_type=F32)


def _dot_nt(a, b):
    return lax.dot_general(a, b, (((1,), (1,)), ((), ())), preferred_element_type=F32)


def _cparams(*sem):
    return pltpu.CompilerParams(dimension_semantics=sem, vmem_limit_bytes=VMEM_LIMIT)


def rel_bucket(dist):
    n = jnp.maximum(dist, 0)
    max_exact = REL_BUCKETS // 2
    nf = jnp.maximum(n, 1).astype(F32)
    large = max_exact + (jnp.log(nf / max_exact) / math.log(REL_MAX_DIST / max_exact)
                         * (REL_BUCKETS - max_exact)).astype(jnp.int32)
    large = jnp.minimum(large, REL_BUCKETS - 1)
    return jnp.where(n < max_exact, n, large)


def _near_bias(rel_table, tq):
    assert tq >= REL_MAX_DIST
    r = jnp.arange(tq)[:, None]
    c = jnp.arange(tq)[None, :]
    d0 = r - c
    b0 = rel_table[rel_bucket(d0)].astype(F32)
    b1 = rel_table[rel_bucket(d0 + tq)].astype(F32)
    return jnp.stack([b0, b1], 0).transpose(3, 0, 1, 2), d0 >= 0


def _mm_kernel(*refs, norm, residual, emit_xn):
    it = iter(refs)
    x_ref = next(it)
    g_ref = next(it) if norm else None
    w_ref = next(it)
    r_ref = next(it) if residual else None
    o_ref = next(it)
    xo_ref = next(it) if emit_xn else None
    xn_ref = next(it)

    @pl.when(pl.program_id(1) == 0)
    def _():
        x = x_ref[...].astype(F32)
        if norm:
            x = x * lax.rsqrt(jnp.mean(x * x, axis=-1, keepdims=True) + EPS) * g_ref[...]
        xn_ref[...] = x.astype(BF16)
        if emit_xn:
            xo_ref[...] = x.astype(BF16)

    acc = _dot(xn_ref[...], w_ref[...])
    if residual:
        acc = acc + r_ref[...]
    o_ref[...] = acc.astype(o_ref.dtype)


def matmul(x, w, *, gain=None, residual=None, emit_xn=False, out_dtype=F32, tm=512, tn=512):
    t, k = x.shape
    n = w.shape[1]
    assert t % tm == 0 and n % tn == 0 and w.shape[0] == k
    norm = gain is not None
    res = residual is not None
    in_specs = [pl.BlockSpec((tm, k), lambda i, j: (i, 0))]
    args = [x]
    if norm:
        in_specs.append(pl.BlockSpec((1, k), lambda i, j: (0, 0)))
        args.append(gain.reshape(1, k).astype(F32))
    in_specs.append(pl.BlockSpec((k, tn), lambda i, j: (0, j)))
    args.append(w)
    if res:
        in_specs.append(pl.BlockSpec((tm, tn), lambda i, j: (i, j)))
        args.append(residual)
    out_specs = [pl.BlockSpec((tm, tn), lambda i, j: (i, j))]
    out_shape = [jax.ShapeDtypeStruct((t, n), out_dtype)]
    if emit_xn:
        out_specs.append(pl.BlockSpec((tm, k), lambda i, j: (i, 0)))
        out_shape.append(jax.ShapeDtypeStruct((t, k), BF16))
    outs = pl.pallas_call(
        functools.partial(_mm_kernel, norm=norm, residual=res, emit_xn=emit_xn),
        grid=(t // tm, n // tn),
        in_specs=in_specs,
        out_specs=out_specs,
        out_shape=out_shape,
        scratch_shapes=[pltpu.VMEM((tm, k), BF16)],
        compiler_params=_cparams("parallel", "arbitrary"),
        name="matmul",
    )(*args)
    return outs if emit_xn else outs[0]


def _diffattn_kernel(far_ref, q_ref, k_ref, v_ref, bias_ref, lam_ref, g_ref, o_ref,
                     m_sc, l_sc, acc_sc, *, tq, lam_init):
    h = pl.program_id(1)
    qi = pl.program_id(2)
    q = (q_ref[...] * (A_DH ** -0.5)).astype(BF16)
    qs = (q[:, :A_DH], q[:, A_DH:])
    m_sc[...] = jnp.full(m_sc.shape, NEG, F32)
    l_sc[...] = jnp.zeros(l_sc.shape, F32)
    acc_sc[...] = jnp.zeros(acc_sc.shape, F32)

    def process(j, bias):
        rows = pl.ds(pl.multiple_of(j * tq, tq), tq)
        kb = k_ref[rows, :].astype(BF16)
        vb = v_ref[rows, :].astype(BF16)
        for m in range(2):
            s = _dot_nt(qs[m], kb[:, m * A_DH:(m + 1) * A_DH]) + bias
            m_prev = m_sc[m]
            m_new = jnp.maximum(m_prev, jnp.max(s, axis=-1, keepdims=True))
            alpha = jnp.exp(m_prev - m_new)
            p = jnp.exp(s - m_new)
            l_sc[m] = alpha * l_sc[m] + jnp.sum(p, axis=-1, keepdims=True)
            acc_sc[m] = alpha * acc_sc[m] + _dot(p.astype(BF16), vb)
            m_sc[m] = m_new

    far = far_ref[h]

    def far_body(j, c):
        process(j, far)
        return c

    lax.fori_loop(0, jnp.maximum(qi - 1, 0), far_body, 0)

    @pl.when(qi > 0)
    def _():
        process(qi - 1, bias_ref[0, 1])

    process(qi, bias_ref[0, 0])

    lp = lam_ref[...]
    lam = (jnp.exp(jnp.sum(lp[0:1] * lp[1:2], keepdims=True))
           - jnp.exp(jnp.sum(lp[2:3] * lp[3:4], keepdims=True)) + lam_init)
    o = acc_sc[0] / l_sc[0] - lam * (acc_sc[1] / l_sc[1])
    o = o * lax.rsqrt(jnp.mean(o * o, axis=-1, keepdims=True) + EPS) * g_ref[...] * (1.0 - lam_init)
    o_ref[...] = o.astype(o_ref.dtype)


def diff_attention(proj, rel_table, lam_p, a_norm, bsz, s_len, layer, tq=256):
    t = bsz * s_len
    nq = s_len // tq
    lam_init = 0.8 - 0.6 * math.exp(-0.3 * layer)
    bias, causal = _near_bias(rel_table, tq)
    bias = bias.at[:, 0].set(jnp.where(causal[None], bias[:, 0], NEG))
    far = rel_table[REL_BUCKETS - 1].astype(F32)
    kcol = A_HEADS * 2 * A_DH // LANES
    return pl.pallas_call(
        functools.partial(_diffattn_kernel, tq=tq, lam_init=lam_init),
        grid=(bsz, A_HEADS, nq),
        in_specs=[
            pl.BlockSpec(memory_space=pltpu.SMEM),
            pl.BlockSpec((tq, LANES), lambda b, h, i: (b * nq + i, h)),
            pl.BlockSpec((s_len, LANES), lambda b, h, i: (b, kcol + h)),
            pl.BlockSpec((s_len, LANES), lambda b, h, i: (b, 2 * kcol + h)),
            pl.BlockSpec((1, 2, tq, tq), lambda b, h, i: (h, 0, 0, 0)),
            pl.BlockSpec((4, A_DH), lambda b, h, i: (0, 0)),
            pl.BlockSpec((1, A_DV), lambda b, h, i: (0, 0)),
        ],
        out_specs=pl.BlockSpec((tq, LANES), lambda b, h, i: (b * nq + i, h)),
        out_shape=jax.ShapeDtypeStruct((t, A_HEADS * A_DV), BF16),
        scratch_shapes=[pltpu.VMEM((2, tq, 1), F32), pltpu.VMEM((2, tq, 1), F32),
                        pltpu.VMEM((2, tq, A_DV), F32)],
        compiler_params=_cparams("parallel", "parallel", "arbitrary"),
        name="diff_attention",
    )(far, proj, proj, proj, bias, lam_p.astype(F32), a_norm.reshape(1, A_DV).astype(F32))


def _retention_kernel(q_ref, k_ref, v_ref, gate_ref, cos_ref, sin_ref, inner_ref, qdec_ref, kdec_ref,
                      cdec_ref, g_ref, o_ref, *, nchunks):
    c = RET_CHUNK
    half = B_DK // 2
    lane = lax.broadcasted_iota(jnp.int32, (c, LANES), 1)
    first = (lane % B_DK) < half

    def rope(x, cos, sin):
        rolled = jnp.where(first, pltpu.roll(x, LANES - half, axis=1), pltpu.roll(x, half, axis=1))
        return x * cos + rolled * sin

    def body(ci, states):
        rows = pl.ds(pl.multiple_of(ci * c, c), c)
        cos = cos_ref[rows, :]
        sin = sin_ref[rows, :]
        qr = rope(q_ref[rows, :], cos, sin) * (B_DK ** -0.5)
        kr = rope(k_ref[rows, :], cos, sin)
        qd = (qr * qdec_ref[0]).astype(BF16)
        kdt = (kr * kdec_ref[0]).T.astype(BF16)
        qb = qr.astype(BF16)
        kb = kr.astype(BF16)
        new_states = []
        for hh in range(2):
            sl = slice(hh * B_DK, (hh + 1) * B_DK)
            vs = slice(hh * B_DV, (hh + 1) * B_DV)
            vv = v_ref[rows, vs].astype(BF16)
            sc = _dot_nt(qb[:, sl], kb[:, sl]) * inner_ref[hh]
            o = _dot(sc.astype(BF16), vv) + _dot(qd[:, sl], states[hh].astype(BF16))
            new_states.append(states[hh] * cdec_ref[hh][0:1, :] + _dot(kdt[sl, :], vv))
            o = o * lax.rsqrt(jnp.mean(o * o, axis=-1, keepdims=True) + EPS) * g_ref[...]
            gt = gate_ref[rows, vs]
            o_ref[rows, vs] = (o * (gt * jax.nn.sigmoid(gt))).astype(o_ref.dtype)
        return tuple(new_states)

    zero = jnp.zeros((B_DK, B_DV), F32)
    lax.fori_loop(0, nchunks, body, (zero, zero))


def retention(proj, b_norm, bsz, s_len):
    t = bsz * s_len
    c = RET_CHUNK
    nh = B_HEADS
    log_gamma = jnp.log(1.0 - 2.0 ** (-5.0 - jnp.arange(nh, dtype=F32)))
    idx = jnp.arange(c, dtype=F32)
    rel = idx[:, None] - idx[None, :]
    inner = jnp.where(rel[None] >= 0, jnp.exp(rel[None] * log_gamma[:, None, None]), 0.0)
    q_decay = jnp.exp((idx[:, None] + 1.0) * log_gamma[None, :])
    k_decay = jnp.exp((c - 1.0 - idx[:, None]) * log_gamma[None, :])
    chunk_decay = jnp.exp(c * log_gamma)

    def pair_lanes(d):
        return jnp.repeat(d.T.reshape(nh // 2, 2, c).transpose(0, 2, 1), B_DK, axis=-1)

    cdec = jnp.broadcast_to(chunk_decay[:, None, None], (nh, 8, B_DV))
    inv = ROPE_BASE ** (-jnp.arange(0, B_DK, 2, dtype=F32) / B_DK)
    ang = jnp.arange(s_len, dtype=F32)[:, None] * inv[None, :]
    cos = jnp.tile(jnp.cos(ang), (1, 4))
    sin = jnp.tile(jnp.concatenate([-jnp.sin(ang), jnp.sin(ang)], axis=-1), (1, 2))
    base = (A_HEADS * 2 * A_DH * 2 + A_HEADS * A_DV) // LANES
    kblk = base + B_HEADS * B_DK // LANES
    vblk = (kblk + B_HEADS * B_DK // LANES) // 2
    gblk = vblk + B_HEADS * B_DV // (2 * LANES)
    return pl.pallas_call(
        functools.partial(_retention_kernel, nchunks=s_len // c),
        grid=(bsz, nh // 2),
        in_specs=[
            pl.BlockSpec((s_len, LANES), lambda b, p: (b, base + p)),
            pl.BlockSpec((s_len, LANES), lambda b, p: (b, kblk + p)),
            pl.BlockSpec((s_len, 2 * B_DV), lambda b, p: (b, vblk + p)),
            pl.BlockSpec((s_len, 2 * B_DV), lambda b, p: (b, gblk + p)),
            pl.BlockSpec((s_len, LANES), lambda b, p: (0, 0)),
            pl.BlockSpec((s_len, LANES), lambda b, p: (0, 0)),
            pl.BlockSpec((2, c, c), lambda b, p: (p, 0, 0)),
            pl.BlockSpec((1, c, LANES), lambda b, p: (p, 0, 0)),
            pl.BlockSpec((1, c, LANES), lambda b, p: (p, 0, 0)),
            pl.BlockSpec((2, 8, B_DV), lambda b, p: (p, 0, 0)),
            pl.BlockSpec((1, B_DV), lambda b, p: (0, 0)),
        ],
        out_specs=pl.BlockSpec((s_len, 2 * B_DV), lambda b, p: (b, p)),
        out_shape=jax.ShapeDtypeStruct((t, nh * B_DV), BF16),
        compiler_params=_cparams("parallel", "parallel"),
        name="retention",
    )(proj, proj, proj, proj, cos, sin, inner, pair_lanes(q_decay), pair_lanes(k_decay), cdec,
      b_norm.reshape(1, B_DV).astype(F32))


def _conv_kernel(x_ref, w_ref, b_ref, o_ref):
    x = x_ref[...]
    row = lax.broadcasted_iota(jnp.int32, x.shape, 0)
    acc = x * w_ref[C_CONV - 1:C_CONV, :] + b_ref[...]
    for j in range(1, C_CONV):
        xs = jnp.where(row >= j, pltpu.roll(x, j, axis=0), 0.0)
        acc = acc + xs * w_ref[C_CONV - 1 - j:C_CONV - j, :]
    o_ref[...] = acc * jax.nn.sigmoid(acc)


def conv_silu(proj, conv_w, conv_b, bsz, s_len, tc=512):
    t = bsz * s_len
    off = CD_XBC // tc
    return pl.pallas_call(
        _conv_kernel,
        grid=(bsz, C_CONV_CH // tc),
        in_specs=[pl.BlockSpec((s_len, tc), lambda b, j: (b, off + j)),
                  pl.BlockSpec((C_CONV, tc), lambda b, j: (0, j)),
                  pl.BlockSpec((1, tc), lambda b, j: (0, j))],
        out_specs=pl.BlockSpec((s_len, tc), lambda b, j: (b, j)),
        out_shape=jax.ShapeDtypeStruct((t, C_CONV_CH), F32),
        compiler_params=_cparams("parallel", "parallel"),
        name="conv_silu",
    )(proj, conv_w.astype(F32), conv_b.reshape(1, C_CONV_CH).astype(F32))


def _ssd_kernel(x_ref, b_ref, c_ref, z_ref, dt_ref, dtb_ref, alog_ref, dsk_ref, ng_ref, o_ref,
                st_sc, y_sc, *, nchunks):
    qn = SSD_CHUNK
    p = C_HEADDIM
    r = lax.broadcasted_iota(jnp.int32, (qn, qn), 0)
    cc = lax.broadcasted_iota(jnp.int32, (qn, qn), 1)
    causal = cc <= r
    t1 = jnp.where(causal, 1.0, 0.0)
    t2 = jnp.where(r > cc, 1.0, 0.0)
    a = -jnp.exp(alog_ref[0])
    dsk = dsk_ref[0]
    st_sc[...] = jnp.zeros(st_sc.shape, F32)

    def body(ci, carry):
        rows = pl.ds(pl.multiple_of(ci * qn, qn), qn)
        xc = x_ref[rows, :]
        bc = b_ref[rows, :]
        cm = c_ref[rows, :]
        dtr = dt_ref[0, 0, rows, :] + dtb_ref[0]
        dt = jnp.maximum(dtr, 0.0) + jnp.log(1.0 + jnp.exp(-jnp.abs(dtr)))
        dta = dt * a
        cmb = cm.astype(BF16)
        cb = _dot_nt(cmb, bc.astype(BF16))
        bt = bc.T.astype(BF16)
        for h in range(C_HPG):
            col = dta[:, h:h + 1]
            seg = jnp.dot(t1, col * t2, precision=HIGHEST, preferred_element_type=F32)
            lm = jnp.where(causal, jnp.exp(seg), 0.0)
            cs = seg[:, 0:1] + col[0:1, :]
            xh = xc[:, h * p:(h + 1) * p]
            xdt = xh * dt[:, h:h + 1]
            st = st_sc[h]
            y = _dot((cb * lm).astype(BF16), xdt.astype(BF16))
            y = y + _dot(cmb, st.astype(BF16)) * jnp.exp(cs)
            last = cs[qn - 1:qn, :]
            st_sc[h] = st * jnp.exp(last) + _dot(bt, (xdt * jnp.exp(last - cs)).astype(BF16))
            y_sc[:, h * p:(h + 1) * p] = y + xh * dsk[:, h:h + 1]
        zz = z_ref[rows, :]
        y = y_sc[...] * (zz * jax.nn.sigmoid(zz))
        y = y * lax.rsqrt(jnp.mean(y * y, axis=-1, keepdims=True) + EPS) * ng_ref[...]
        o_ref[rows, :] = y.astype(o_ref.dtype)
        return carry

    lax.fori_loop(0, nchunks, body, 0)


def ssd_mixer(proj, xconv, dt_bias, a_log, d_skip, norm_g, bsz, s_len):
    t = bsz * s_len
    g = C_GROUPS
    gw = C_DINNER // g
    dtg = proj[:, CD_MISC:CD_MISC + C_HEADS].reshape(bsz, s_len, g, C_HPG).transpose(0, 2, 1, 3)
    per_group = lambda v: v.astype(F32).reshape(g, 1, C_HPG)
    nb = C_DINNER // LANES
    return pl.pallas_call(
        functools.partial(_ssd_kernel, nchunks=s_len // SSD_CHUNK),
        grid=(bsz, g),
        in_specs=[
            pl.BlockSpec((s_len, gw), lambda b, k: (b, k)),
            pl.BlockSpec((s_len, C_DSTATE), lambda b, k: (b, nb + k)),
            pl.BlockSpec((s_len, C_DSTATE), lambda b, k: (b, nb + g + k)),
            pl.BlockSpec((s_len, gw), lambda b, k: (b, k)),
            pl.BlockSpec((1, 1, s_len, C_HPG), lambda b, k: (b, k, 0, 0)),
            pl.BlockSpec((1, 1, C_HPG), lambda b, k: (k, 0, 0)),
            pl.BlockSpec((1, 1, C_HPG), lambda b, k: (k, 0, 0)),
            pl.BlockSpec((1, 1, C_HPG), lambda b, k: (k, 0, 0)),
            pl.BlockSpec((1, gw), lambda b, k: (0, k)),
        ],
        out_specs=pl.BlockSpec((s_len, gw), lambda b, k: (b, k)),
        out_shape=jax.ShapeDtypeStruct((t, C_DINNER), BF16),
        scratch_shapes=[pltpu.VMEM((C_HPG, C_DSTATE, C_HEADDIM), F32),
                        pltpu.VMEM((SSD_CHUNK, gw), F32)],
        compiler_params=_cparams("parallel", "parallel"),
        name="ssd",
    )(xconv, xconv, xconv, proj, dtg, per_group(dt_bias), per_group(a_log), per_group(d_skip),
      norm_g.reshape(1, C_DINNER).astype(F32))


def _dsa_kernel(far_ref, q_ref, qidx_ref, misc_ref, ckv_ref, kidx_ref, wuk_ref, wuv_ref, kvn_ref,
                band_ref, o_ref, ckvn_sc, kidx_sc, key_sc, lg_sc, mask_sc, *, s_len, topk):
    tq = Q_BLOCK
    qi = pl.program_id(1)

    @pl.when(qi == 0)
    def _():
        c = ckv_ref[...]
        ckvn_sc[...] = (c * lax.rsqrt(jnp.mean(c * c, axis=-1, keepdims=True) + EPS)
                        * kvn_ref[...]).astype(BF16)
        kidx_sc[...] = kidx_ref[:, :IDX_DIM].astype(BF16)

    qidx = qidx_ref[...].astype(BF16)
    w = misc_ref[:, MISC_WIDX:MISC_WIDX + IDX_HEADS] * ((IDX_HEADS * IDX_DIM) ** -0.5)
    kx = kidx_sc[...]
    sc = jnp.zeros((tq, s_len), F32)
    for hi in range(IDX_HEADS):
        rel = _dot_nt(qidx[:, hi * IDX_DIM:(hi + 1) * IDX_DIM], kx)
        sc = sc + jnp.maximum(rel, 0.0) * w[:, hi:hi + 1]
    col = lax.broadcasted_iota(jnp.int32, (tq, s_len), 1)
    row = lax.broadcasted_iota(jnp.int32, (tq, s_len), 0) + qi * tq
    causal = col <= row
    sc = jnp.where(causal, sc, -jnp.inf)
    bits = pltpu.bitcast(sc, jnp.int32)
    key_sc[...] = jnp.where(bits < 0, bits ^ jnp.int32(0x7FFFFFFF), bits)
    kf = float(topk)
    int_min = jnp.int32(-2 ** 31)

    def count_ge(cand):
        return jnp.sum(jnp.where(key_sc[...] >= cand, 1.0, 0.0), axis=-1, keepdims=True)

    zero = jnp.zeros((tq, 1), jnp.int32)
    prefix = jnp.where(count_ge(zero) >= kf, zero, zero + int_min)

    def bisect(i, prefix):
        cand = prefix | jnp.left_shift(jnp.int32(1), 30 - i)
        return jnp.where(count_ge(cand) >= kf, cand, prefix)

    thr = lax.fori_loop(0, 31, bisect, prefix)
    key = key_sc[...]
    gt = key > thr
    eq = key == thr
    need = kf - jnp.sum(jnp.where(gt, 1.0, 0.0), axis=-1, keepdims=True)
    ur = lax.broadcasted_iota(jnp.int32, (LANES, LANES), 0)
    uc = lax.broadcasted_iota(jnp.int32, (LANES, LANES), 1)
    upper = jnp.where(ur <= uc, 1.0, 0.0).astype(BF16)
    eqf = jnp.where(eq, 1.0, 0.0)
    carry = jnp.zeros((tq, 1), F32)
    for j in range(s_len // LANES):
        ls = slice(j * LANES, (j + 1) * LANES)
        e = eqf[:, ls]
        run = _dot(e.astype(BF16), upper) + carry
        carry = carry + jnp.sum(e, axis=-1, keepdims=True)
        take = jnp.where(gt[:, ls], 1.0, jnp.where(run <= need, e, 0.0))
        mask_sc[:, ls] = jnp.where(causal[:, ls], jnp.where(take > 0.5, 0.0, NEG), NEG)

    q = q_ref[...].astype(BF16)
    ck = ckvn_sc[...]
    d0 = pl.ds(pl.multiple_of(qi * tq, tq), tq)
    d1 = pl.ds(pl.multiple_of(jnp.maximum(qi - 1, 0) * tq, tq), tq)
    for h in range(D_HEADS):
        qa = _dot(q[:, h * D_DH:(h + 1) * D_DH], wuk_ref[h])
        lg_sc[...] = _dot_nt(qa.astype(BF16), ck) * (D_DH ** -0.5) + far_ref[h]
        lg_sc[:, d0] += band_ref[h, 0]

        @pl.when(qi > 0)
        def _():
            lg_sc[:, d1] += band_ref[h, 1]

        lg = lg_sc[...] + mask_sc[...]
        m = jnp.max(lg, axis=-1, keepdims=True)
        pr = jnp.exp(lg - m)
        ctx = _dot(pr.astype(BF16), ck) / jnp.sum(pr, axis=-1, keepdims=True)
        o_ref[:, h * D_DH:(h + 1) * D_DH] = _dot(ctx.astype(BF16), wuv_ref[h]).astype(o_ref.dtype)


def dsa_mixer(proj, kv_norm, w_uk, w_uv, rel_table, bsz, s_len):
    t = bsz * s_len
    tq = Q_BLOCK
    nq = s_len // tq
    topk = min(IDX_TOPK_MAX, s_len // 4)
    bias, causal = _near_bias(rel_table, tq)
    far = rel_table[REL_BUCKETS - 1].astype(F32)
    band = bias - far[:, None, None, None]
    band = band.at[:, 0].set(jnp.where(causal[None], band[:, 0], 0.0))
    hw = D_HEADS * D_DH
    return pl.pallas_call(
        functools.partial(_dsa_kernel, s_len=s_len, topk=topk),
        grid=(bsz, nq),
        in_specs=[
            pl.BlockSpec(memory_space=pltpu.SMEM),
            pl.BlockSpec((tq, hw), lambda b, i: (b * nq + i, CD_Q // hw)),
            pl.BlockSpec((tq, hw), lambda b, i: (b * nq + i, CD_QIDX // hw)),
            pl.BlockSpec((tq, LANES), lambda b, i: (b * nq + i, CD_MISC // LANES)),
            pl.BlockSpec((s_len, D_LATENT), lambda b, i: (b, CD_CKV // D_LATENT)),
            pl.BlockSpec((s_len, LANES), lambda b, i: (b, CD_KIDX // LANES)),
            pl.BlockSpec((D_HEADS, D_DH, D_LATENT), lambda b, i: (0, 0, 0)),
            pl.BlockSpec((D_HEADS, D_LATENT, D_DH), lambda b, i: (0, 0, 0)),
            pl.BlockSpec((1, D_LATENT), lambda b, i: (0, 0)),
            pl.BlockSpec((D_HEADS, 2, tq, tq), lambda b, i: (0, 0, 0, 0)),
        ],
        out_specs=pl.BlockSpec((tq, hw), lambda b, i: (b * nq + i, 0)),
        out_shape=jax.ShapeDtypeStruct((t, hw), BF16),
        scratch_shapes=[pltpu.VMEM((s_len, D_LATENT), BF16), pltpu.VMEM((s_len, IDX_DIM), BF16),
                        pltpu.VMEM((tq, s_len), jnp.int32), pltpu.VMEM((tq, s_len), F32),
                        pltpu.VMEM((tq, s_len), F32)],
        compiler_params=_cparams("parallel", "arbitrary"),
        name="dsa",
    )(far, proj, proj, proj, proj, proj, w_uk.astype(BF16), w_uv.astype(BF16),
      kv_norm.reshape(1, D_LATENT).astype(F32), band)


def _peer_tables():
    pairs = [(k1, k2) for k1 in range(PEER_TOPK) for k2 in range(PEER_TOPK)
             if (k1 + 1) * (k2 + 1) <= PEER_TOPK]
    n = PEER_NKEYS
    r1 = np.zeros((n, n), np.float32)
    r2 = np.zeros((n, n), np.float32)
    pad = np.full((n, 1), NEG, np.float32)
    for r, (k1, k2) in enumerate(pairs):
        r1[r, k1] = 1.0
        r2[r, k2] = 1.0
        pad[r, 0] = 0.0
    return jnp.asarray(r1), jnp.asarray(r2), jnp.asarray(pad), jnp.asarray(r1.T, dtype=BF16)


def _peer_select_kernel(q_ref, keys_ref, r1_ref, r2_ref, pad_ref, grp_ref,
                        rank2_ref, e2_ref, n_ref, coef_ref):
    tm = q_ref.shape[0]
    nk = PEER_NKEYS
    ridx = lax.broadcasted_iota(jnp.int32, (nk, tm), 0).astype(F32)

    def extract(s, track):
        rank = jnp.full((nk, tm), 99.0, F32)
        vals = jnp.zeros((nk, tm), F32)
        for k in range(PEER_TOPK):
            m = jnp.max(s, axis=0, keepdims=True)
            first = jnp.min(jnp.where(s == m, ridx, 1e9), axis=0, keepdims=True)
            hit = ridx == first
            s = jnp.where(hit, -jnp.inf, s)
            if track:
                rank = jnp.where(hit, float(k), rank)
                vals = jnp.where(ridx == float(k), m, vals)
        return s, rank, vals

    q = q_ref[...].astype(BF16)
    half = PEER_DQ // 2
    s1 = _dot_nt(keys_ref[0, 0], q[:, :half])
    s2 = _dot_nt(keys_ref[0, 1], q[:, half:])
    _, rank1, a1 = extract(s1, True)
    _, rank2, a2 = extract(s2, True)
    cand0 = (jnp.dot(r1_ref[...], a1, precision=HIGHEST, preferred_element_type=F32)
             + jnp.dot(r2_ref[...], a2, precision=HIGHEST, preferred_element_type=F32) + pad_ref[...])
    cand, _, _ = extract(cand0, False)
    taken = (cand == -jnp.inf) & (cand0 > 0.5 * NEG)
    cnt = _dot(grp_ref[...], jnp.where(taken, 1.0, 0.0).astype(BF16))
    top = a1[0:1, :] + a2[0:1, :]
    zsum = jnp.sum(jnp.where(taken, jnp.exp(cand0 - top), 0.0), axis=0, keepdims=True)
    n_i = jnp.zeros((nk, tm), F32)
    for k in range(PEER_TOPK):
        n_i = jnp.where(rank1 == float(k), cnt[k:k + 1, :], n_i)
    rank2_ref[0] = rank2
    e2_ref[0] = jnp.exp(s2 - a2[0:1, :])
    n_ref[0] = n_i
    coef_ref[0] = jnp.exp(s1 - a1[0:1, :]) / zsum


def peer_select(q, keys, tm=128):
    t = q.shape[0]
    r1, r2, pad, grp = _peer_tables()
    nk = PEER_NKEYS
    full = lambda shape: pl.BlockSpec(shape, lambda i, h: (0,) * len(shape))
    out_spec = pl.BlockSpec((1, nk, tm), lambda i, h: (h, 0, i))
    out_sds = jax.ShapeDtypeStruct((PEER_HEADS, nk, t), F32)
    return pl.pallas_call(
        _peer_select_kernel,
        grid=(t // tm, PEER_HEADS),
        in_specs=[pl.BlockSpec((tm, PEER_DQ), lambda i, h: (i, h)),
                  pl.BlockSpec((1, 2, nk, PEER_DQ // 2), lambda i, h: (h, 0, 0, 0)),
                  full((nk, nk)), full((nk, nk)), full((nk, 1)), full((nk, nk))],
        out_specs=[out_spec] * 4,
        out_shape=[out_sds] * 4,
        compiler_params=_cparams("parallel", "parallel"),
        name="peer_select",
    )(q, keys.astype(BF16), r1, r2, pad, grp)


def _peer_dense_kernel(xn_ref, u_ref, vt_ref, rank2_ref, e2_ref, n_ref, coef_ref, h_ref, o_ref,
                       acc_sc, z_sc, *, et, tm):
    e = pl.program_id(1)

    @pl.when(e == 0)
    def _():
        acc_sc[...] = jnp.zeros(acc_sc.shape, F32)

    xn = xn_ref[...]
    nk = PEER_NKEYS
    for r in range(et // nk):
        y = _dot_nt(u_ref[r * nk:(r + 1) * nk, :], xn)
        for lc in range(tm // LANES):
            ls = slice(lc * LANES, (lc + 1) * LANES)
            w = jnp.zeros((nk, LANES), F32)
            for h in range(PEER_HEADS):
                w = w + jnp.where(rank2_ref[h, :, ls] < n_ref[h, 0, r:r + 1, ls],
                                  coef_ref[h, 0, r:r + 1, ls] * e2_ref[h, :, ls], 0.0)
            yy = y[:, ls]
            act = 0.5 * yy * (1.0 + jnp.tanh(0.7978845608028654 * (yy + 0.044715 * (yy * yy * yy))))
            z_sc[r * nk:(r + 1) * nk, ls] = (act * w).astype(BF16)
    acc_sc[...] += _dot(vt_ref[...], z_sc[...])

    @pl.when(e == pl.num_programs(1) - 1)
    def _():
        o_ref[...] = h_ref[...] + acc_sc[...].T


def peer_dense(h, xn, u, vt, rank2, e2, n_i, coef, tm=512, et=512):
    t, d = h.shape
    ne = u.shape[0]
    ipb = et // PEER_NKEYS
    n4 = n_i.reshape(PEER_HEADS, PEER_NKEYS // ipb, ipb, t)
    c4 = coef.reshape(PEER_HEADS, PEER_NKEYS // ipb, ipb, t)
    tok = pl.BlockSpec((PEER_HEADS, PEER_NKEYS, tm), lambda i, e: (0, 0, i))
    per_i = pl.BlockSpec((PEER_HEADS, 1, ipb, tm), lambda i, e: (0, e, 0, i))
    return pl.pallas_call(
        functools.partial(_peer_dense_kernel, et=et, tm=tm),
        grid=(t // tm, ne // et),
        in_specs=[pl.BlockSpec((tm, d), lambda i, e: (i, 0)),
                  pl.BlockSpec((et, d), lambda i, e: (e, 0)),
                  pl.BlockSpec((d, et), lambda i, e: (0, e)),
                  tok, tok, per_i, per_i,
                  pl.BlockSpec((tm, d), lambda i, e: (i, 0))],
        out_specs=pl.BlockSpec((tm, d), lambda i, e: (i, 0)),
        out_shape=jax.ShapeDtypeStruct((t, d), F32),
        scratch_shapes=[pltpu.VMEM((d, tm), F32), pltpu.VMEM((et, tm), BF16)],
        compiler_params=_cparams("parallel", "arbitrary"),
        name="peer_dense",
    )(xn, u, vt, rank2, e2, n4, c4, h)


def peer_layer(h, norm_g, w_q, keys, u, v):
    q, xn = matmul(h, w_q.astype(BF16), gain=norm_g, emit_xn=True)
    rank2, e2, n_i, coef = peer_select(q, keys)
    return peer_dense(h, xn, u.astype(BF16), v.T.astype(BF16), rank2, e2, n_i, coef)


def _rmsnorm_kernel(x_ref, g_ref, o_ref):
    x = x_ref[...]
    o_ref[...] = x * lax.rsqrt(jnp.mean(x * x, axis=-1, keepdims=True) + EPS) * g_ref[...]


def rmsnorm_final(x2d, g, tm=512):
    t, d = x2d.shape
    return pl.pallas_call(
        _rmsnorm_kernel,
        grid=(t // tm,),
        in_specs=[pl.BlockSpec((tm, d), lambda i: (i, 0)),
                  pl.BlockSpec((1, d), lambda i: (0, 0))],
        out_specs=pl.BlockSpec((tm, d), lambda i: (i, 0)),
        out_shape=jax.ShapeDtypeStruct((t, d), F32),
        compiler_params=_cparams("parallel"),
        name="rmsnorm_final",
    )(x2d, g.reshape(1, d).astype(F32))


def _ab_w_in(w):
    perm = np.concatenate([np.arange(0, B_DK, 2), np.arange(1, B_DK, 2)])
    start = A_HEADS * 2 * A_DH * 2 + A_HEADS * A_DV
    cols = np.arange(w.shape[1])
    for sec in range(2):
        for hd in range(B_HEADS):
            o = start + sec * B_HEADS * B_DK + hd * B_DK
            cols[o:o + B_DK] = o + perm
    return w[:, cols].astype(BF16)


def _cd_w_in(w):
    widths = (C_DINNER, C_CONV_CH, C_HEADS, D_HEADS * D_DH, D_LATENT, IDX_HEADS * IDX_DIM, IDX_DIM, IDX_HEADS)
    offs = np.concatenate([[0], np.cumsum(widths)])
    z, xbc, dt, q, ckv, qidx, kidx, widx = [w[:, offs[i]:offs[i + 1]] for i in range(8)]
    k = w.shape[0]
    pad = lambda n: jnp.zeros((k, n), w.dtype)
    out = jnp.concatenate([z, xbc, q, qidx, ckv, kidx, pad(LANES - IDX_DIM),
                           dt, widx, pad(LANES - C_HEADS - IDX_HEADS)], axis=1)
    assert out.shape[1] == CD_WIDTH
    return out.astype(BF16)


def kernel(x, rel_table, ab_w_in, ab_w_out, ab_lambda, ab_a_norm, ab_b_norm, cd_w_in, cd_w_out, cd_conv_w, cd_conv_b, cd_dt_bias, cd_a_log, cd_d_skip, cd_ssm_norm, cd_kv_norm, cd_w_uk, cd_w_uv, norm_mix, norm_ffn, peer_w_q, peer_keys, peer_u, peer_v, norm_final):
    bsz, s_len, d = x.shape
    h = x.reshape(bsz * s_len, d)
    for layer in range(DEPTH):
        i = layer // 2
        if layer % 2 == 0:
            proj = matmul(h, _ab_w_in(ab_w_in[i]), gain=norm_mix[layer])
            oa = diff_attention(proj, rel_table, ab_lambda[i], ab_a_norm[i], bsz, s_len, layer)
            ob = retention(proj, ab_b_norm[i], bsz, s_len)
            mixed = jnp.concatenate([oa, ob], axis=-1)
            h = matmul(mixed, ab_w_out[i].astype(BF16), residual=h)
        else:
            proj = matmul(h, _cd_w_in(cd_w_in[i]), gain=norm_mix[layer])
            xconv = conv_silu(proj, cd_conv_w[i], cd_conv_b[i], bsz, s_len)
            yc = ssd_mixer(proj, xconv, cd_dt_bias[i], cd_a_log[i], cd_d_skip[i], cd_ssm_norm[i],
                           bsz, s_len)
            yd = dsa_mixer(proj, cd_kv_norm[i], cd_w_uk[i], cd_w_uv[i], rel_table, bsz, s_len)
            mixed = jnp.concatenate([yc, yd], axis=-1)
            h = matmul(mixed, cd_w_out[i].astype(BF16), residual=h)
        h = peer_layer(h, norm_ffn[layer], peer_w_q[layer], peer_keys[layer], peer_u[layer],
                       peer_v[layer])
    return rmsnorm_final(h, norm_final).reshape(bsz, s_len, d)
```

```python
import functools
import math

import jax
import jax.numpy as jnp
import numpy as np
from jax import lax
from jax.experimental import pallas as pl
from jax.experimental.pallas import tpu as pltpu

D_MODEL = 2048
DEPTH = 2
EPS = 1e-6
Q_BLOCK = 128
REL_BUCKETS = 32
REL_MAX_DIST = 128
A_HEADS = 8
A_DH = 64
A_DV = 2 * A_DH
B_HEADS = 8
B_DK = 64
B_DV = 128
RET_CHUNK = 128
ROPE_BASE = 10000.0
C_DINNER = D_MODEL
C_HEADDIM = 64
C_HEADS = C_DINNER // C_HEADDIM
C_GROUPS = 4
C_HPG = C_HEADS // C_GROUPS
C_DSTATE = 128
C_CONV = 4
C_CONV_CH = C_DINNER + 2 * C_GROUPS * C_DSTATE
SSD_CHUNK = 128
D_HEADS = 8
D_DH = 128
D_LATENT = 256
IDX_HEADS = 16
IDX_DIM = 64
IDX_TOPK_MAX = 256
PEER_HEADS = 8
PEER_NKEYS = 128
PEER_DQ = 256
PEER_TOPK = 16

F32 = jnp.float32
BF16 = jnp.bfloat16
NEG = -1e30
LANES = 128
VMEM_LIMIT = 56 * 1024 * 1024
HIGHEST = lax.Precision.HIGHEST

CD_Z = 0
CD_XBC = CD_Z + C_DINNER
CD_Q = CD_XBC + C_CONV_CH
CD_QIDX = CD_Q + D_HEADS * D_DH
CD_CKV = CD_QIDX + IDX_HEADS * IDX_DIM
CD_KIDX = CD_CKV + D_LATENT
CD_MISC = CD_KIDX + LANES
CD_WIDTH = CD_MISC + LANES
MISC_WIDX = C_HEADS


def _dot(a, b):
    return jnp.dot(a, b, preferred_element_type=F32)


def _dot_nt(a, b):
    return lax.dot_general(a, b, (((1,), (1,)), ((), ())), preferred_element_type=F32)


def _cparams(*sem):
    return pltpu.CompilerParams(dimension_semantics=sem, vmem_limit_bytes=VMEM_LIMIT)


def rel_bucket(dist):
    n = jnp.maximum(dist, 0)
    max_exact = REL_BUCKETS // 2
    nf = jnp.maximum(n, 1).astype(F32)
    large = max_exact + (jnp.log(nf / max_exact) / math.log(REL_MAX_DIST / max_exact)
                         * (REL_BUCKETS - max_exact)).astype(jnp.int32)
    large = jnp.minimum(large, REL_BUCKETS - 1)
    return jnp.where(n < max_exact, n, large)


def _near_bias(rel_table, tq):
    assert tq >= REL_MAX_DIST
    r = jnp.arange(tq)[:, None]
    c = jnp.arange(tq)[None, :]
    d0 = r - c
    b0 = rel_table[rel_bucket(d0)].astype(F32)
    b1 = rel_table[rel_bucket(d0 + tq)].astype(F32)
    return jnp.stack([b0, b1], 0).transpose(3, 0, 1, 2), d0 >= 0


def _mm_kernel(*refs, norm, residual, emit_xnt):
    it = iter(refs)
    x_ref = next(it)
    g_ref = next(it) if norm else None
    w_ref = next(it)
    r_ref = next(it) if residual else None
    o_ref = next(it)
    xo_ref = next(it) if emit_xnt else None
    xn_ref = next(it)

    @pl.when(pl.program_id(1) == 0)
    def _():
        x = x_ref[...].astype(F32)
        if norm:
            x = x * lax.rsqrt(jnp.mean(x * x, axis=-1, keepdims=True) + EPS) * g_ref[...]
        xn_ref[...] = x.astype(BF16)
        if emit_xnt:
            xo_ref[...] = x.T.astype(BF16)

    acc = _dot(xn_ref[...], w_ref[...])
    if residual:
        acc = acc + r_ref[...]
    o_ref[...] = acc.astype(o_ref.dtype)


def matmul(x, w, *, gain=None, residual=None, emit_xnt=False, out_dtype=F32, tm=1024, tn=512):
    t, k = x.shape
    n = w.shape[1]
    tm = min(tm, t)
    assert t % tm == 0 and n % tn == 0 and w.shape[0] == k
    norm = gain is not None
    res = residual is not None
    in_specs = [pl.BlockSpec((tm, k), lambda i, j: (i, 0))]
    args = [x]
    if norm:
        in_specs.append(pl.BlockSpec((1, k), lambda i, j: (0, 0)))
        args.append(gain.reshape(1, k).astype(F32))
    in_specs.append(pl.BlockSpec((k, tn), lambda i, j: (0, j)))
    args.append(w)
    if res:
        in_specs.append(pl.BlockSpec((tm, tn), lambda i, j: (i, j)))
        args.append(residual)
    out_specs = [pl.BlockSpec((tm, tn), lambda i, j: (i, j))]
    out_shape = [jax.ShapeDtypeStruct((t, n), out_dtype)]
    if emit_xnt:
        out_specs.append(pl.BlockSpec((k, tm), lambda i, j: (0, i)))
        out_shape.append(jax.ShapeDtypeStruct((k, t), BF16))
    outs = pl.pallas_call(
        functools.partial(_mm_kernel, norm=norm, residual=res, emit_xnt=emit_xnt),
        grid=(t // tm, n // tn),
        in_specs=in_specs,
        out_specs=out_specs,
        out_shape=out_shape,
        scratch_shapes=[pltpu.VMEM((tm, k), BF16)],
        compiler_params=_cparams("parallel", "arbitrary"),
        name="matmul",
    )(*args)
    return outs if emit_xnt else outs[0]


def _diffattn_kernel(far_ref, q_ref, k_ref, v_ref, bias_ref, lam_ref, g_ref, o_ref,
                     m_sc, l_sc, acc_sc, *, tq, lam_init):
    h = pl.program_id(1)
    qi = pl.program_id(2)
    q = (q_ref[...] * (A_DH ** -0.5)).astype(BF16)
    qs = (q[:, :A_DH], q[:, A_DH:])
    m_sc[...] = jnp.full(m_sc.shape, NEG, F32)
    l_sc[...] = jnp.zeros(l_sc.shape, F32)
    acc_sc[...] = jnp.zeros(acc_sc.shape, F32)

    def process(j, bias):
        rows = pl.ds(pl.multiple_of(j * tq, tq), tq)
        kb = k_ref[rows, :].astype(BF16)
        vb = v_ref[rows, :].astype(BF16)
        for m in range(2):
            s = _dot_nt(qs[m], kb[:, m * A_DH:(m + 1) * A_DH]) + bias
            m_prev = m_sc[m]
            m_new = jnp.maximum(m_prev, jnp.max(s, axis=-1, keepdims=True))
            alpha = jnp.exp(m_prev - m_new)
            p = jnp.exp(s - m_new)
            l_sc[m] = alpha * l_sc[m] + jnp.sum(p, axis=-1, keepdims=True)
            acc_sc[m] = alpha * acc_sc[m] + _dot(p.astype(BF16), vb)
            m_sc[m] = m_new

    far = far_ref[h]

    def far_body(j, c):
        process(j, far)
        return c

    lax.fori_loop(0, jnp.maximum(qi - 1, 0), far_body, 0)

    @pl.when(qi > 0)
    def _():
        process(qi - 1, bias_ref[0, 1])

    process(qi, bias_ref[0, 0])

    lp = lam_ref[...]
    lam = (jnp.exp(jnp.sum(lp[0:1] * lp[1:2], keepdims=True))
           - jnp.exp(jnp.sum(lp[2:3] * lp[3:4], keepdims=True)) + lam_init)
    o = acc_sc[0] / l_sc[0] - lam * (acc_sc[1] / l_sc[1])
    o = o * lax.rsqrt(jnp.mean(o * o, axis=-1, keepdims=True) + EPS) * g_ref[...] * (1.0 - lam_init)
    o_ref[...] = o.astype(o_ref.dtype)


def diff_attention(proj, rel_table, lam_p, a_norm, bsz, s_len, layer, tq=256):
    t = bsz * s_len
    nq = s_len // tq
    lam_init = 0.8 - 0.6 * math.exp(-0.3 * layer)
    bias, causal = _near_bias(rel_table, tq)
    bias = bias.at[:, 0].set(jnp.where(causal[None], bias[:, 0], NEG))
    far = rel_table[REL_BUCKETS - 1].astype(F32)
    kcol = A_HEADS * 2 * A_DH // LANES
    return pl.pallas_call(
        functools.partial(_diffattn_kernel, tq=tq, lam_init=lam_init),
        grid=(bsz, A_HEADS, nq),
        in_specs=[
            pl.BlockSpec(memory_space=pltpu.SMEM),
            pl.BlockSpec((tq, LANES), lambda b, h, i: (b * nq + i, h)),
            pl.BlockSpec((s_len, LANES), lambda b, h, i: (b, kcol + h)),
            pl.BlockSpec((s_len, LANES), lambda b, h, i: (b, 2 * kcol + h)),
            pl.BlockSpec((1, 2, tq, tq), lambda b, h, i: (h, 0, 0, 0)),
            pl.BlockSpec((4, A_DH), lambda b, h, i: (0, 0)),
            pl.BlockSpec((1, A_DV), lambda b, h, i: (0, 0)),
        ],
        out_specs=pl.BlockSpec((tq, LANES), lambda b, h, i: (b * nq + i, h)),
        out_shape=jax.ShapeDtypeStruct((t, A_HEADS * A_DV), BF16),
        scratch_shapes=[pltpu.VMEM((2, tq, 1), F32), pltpu.VMEM((2, tq, 1), F32),
                        pltpu.VMEM((2, tq, A_DV), F32)],
        compiler_params=_cparams("parallel", "parallel", "arbitrary"),
        name="diff_attention",
    )(far, proj, proj, proj, bias, lam_p.astype(F32), a_norm.reshape(1, A_DV).astype(F32))


def _retention_kernel(q_ref, k_ref, v_ref, gate_ref, cos_ref, sin_ref, inner_ref, qdec_ref, kdec_ref,
                      cdec_ref, g_ref, o_ref, *, nchunks):
    c = RET_CHUNK
    half = B_DK // 2
    lane = lax.broadcasted_iota(jnp.int32, (c, LANES), 1)
    first = (lane % B_DK) < half

    def rope(x, cos, sin):
        rolled = jnp.where(first, pltpu.roll(x, LANES - half, axis=1), pltpu.roll(x, half, axis=1))
        return x * cos + rolled * sin

    def body(ci, states):
        rows = pl.ds(pl.multiple_of(ci * c, c), c)
        cos = cos_ref[rows, :]
        sin = sin_ref[rows, :]
        qr = rope(q_ref[rows, :], cos, sin) * (B_DK ** -0.5)
        kr = rope(k_ref[rows, :], cos, sin)
        qd = (qr * qdec_ref[0]).astype(BF16)
        kdt = (kr * kdec_ref[0]).T.astype(BF16)
        qb = qr.astype(BF16)
        kb = kr.astype(BF16)
        new_states = []
        for hh in range(2):
            sl = slice(hh * B_DK, (hh + 1) * B_DK)
            vs = slice(hh * B_DV, (hh + 1) * B_DV)
            vv = v_ref[rows, vs].astype(BF16)
            sc = _dot_nt(qb[:, sl], kb[:, sl]) * inner_ref[hh]
            o = _dot(sc.astype(BF16), vv) + _dot(qd[:, sl], states[hh].astype(BF16))
            new_states.append(states[hh] * cdec_ref[hh][0:1, :] + _dot(kdt[sl, :], vv))
            o = o * lax.rsqrt(jnp.mean(o * o, axis=-1, keepdims=True) + EPS) * g_ref[...]
            gt = gate_ref[rows, vs]
            o_ref[rows, vs] = (o * (gt * jax.nn.sigmoid(gt))).astype(o_ref.dtype)
        return tuple(new_states)

    zero = jnp.zeros((B_DK, B_DV), F32)
    lax.fori_loop(0, nchunks, body, (zero, zero))


def retention(proj, b_norm, bsz, s_len):
    t = bsz * s_len
    c = RET_CHUNK
    nh = B_HEADS
    log_gamma = jnp.log(1.0 - 2.0 ** (-5.0 - jnp.arange(nh, dtype=F32)))
    idx = jnp.arange(c, dtype=F32)
    rel = idx[:, None] - idx[None, :]
    inner = jnp.where(rel[None] >= 0, jnp.exp(rel[None] * log_gamma[:, None, None]), 0.0)
    q_decay = jnp.exp((idx[:, None] + 1.0) * log_gamma[None, :])
    k_decay = jnp.exp((c - 1.0 - idx[:, None]) * log_gamma[None, :])
    chunk_decay = jnp.exp(c * log_gamma)

    def pair_lanes(d):
        return jnp.repeat(d.T.reshape(nh // 2, 2, c).transpose(0, 2, 1), B_DK, axis=-1)

    cdec = jnp.broadcast_to(chunk_decay[:, None, None], (nh, 8, B_DV))
    inv = ROPE_BASE ** (-jnp.arange(0, B_DK, 2, dtype=F32) / B_DK)
    ang = jnp.arange(s_len, dtype=F32)[:, None] * inv[None, :]
    cos = jnp.tile(jnp.cos(ang), (1, 4))
    sin = jnp.tile(jnp.concatenate([-jnp.sin(ang), jnp.sin(ang)], axis=-1), (1, 2))
    base = (A_HEADS * 2 * A_DH * 2 + A_HEADS * A_DV) // LANES
    kblk = base + B_HEADS * B_DK // LANES
    vblk = (kblk + B_HEADS * B_DK // LANES) // 2
    gblk = vblk + B_HEADS * B_DV // (2 * LANES)
    return pl.pallas_call(
        functools.partial(_retention_kernel, nchunks=s_len // c),
        grid=(bsz, nh // 2),
        in_specs=[
            pl.BlockSpec((s_len, LANES), lambda b, p: (b, base + p)),
            pl.BlockSpec((s_len, LANES), lambda b, p: (b, kblk + p)),
            pl.BlockSpec((s_len, 2 * B_DV), lambda b, p: (b, vblk + p)),
            pl.BlockSpec((s_len, 2 * B_DV), lambda b, p: (b, gblk + p)),
            pl.BlockSpec((s_len, LANES), lambda b, p: (0, 0)),
            pl.BlockSpec((s_len, LANES), lambda b, p: (0, 0)),
            pl.BlockSpec((2, c, c), lambda b, p: (p, 0, 0)),
            pl.BlockSpec((1, c, LANES), lambda b, p: (p, 0, 0)),
            pl.BlockSpec((1, c, LANES), lambda b, p: (p, 0, 0)),
            pl.BlockSpec((2, 8, B_DV), lambda b, p: (p, 0, 0)),
            pl.BlockSpec((1, B_DV), lambda b, p: (0, 0)),
        ],
        out_specs=pl.BlockSpec((s_len, 2 * B_DV), lambda b, p: (b, p)),
        out_shape=jax.ShapeDtypeStruct((t, nh * B_DV), BF16),
        compiler_params=_cparams("parallel", "parallel"),
        name="retention",
    )(proj, proj, proj, proj, cos, sin, inner, pair_lanes(q_decay), pair_lanes(k_decay), cdec,
      b_norm.reshape(1, B_DV).astype(F32))


def _conv_kernel(x_ref, w_ref, b_ref, o_ref):
    x = x_ref[...]
    row = lax.broadcasted_iota(jnp.int32, x.shape, 0)
    acc = x * w_ref[C_CONV - 1:C_CONV, :] + b_ref[...]
    for j in range(1, C_CONV):
        xs = jnp.where(row >= j, pltpu.roll(x, j, axis=0), 0.0)
        acc = acc + xs * w_ref[C_CONV - 1 - j:C_CONV - j, :]
    o_ref[...] = acc * jax.nn.sigmoid(acc)


def conv_silu(proj, conv_w, conv_b, bsz, s_len, tc=512):
    t = bsz * s_len
    off = CD_XBC // tc
    return pl.pallas_call(
        _conv_kernel,
        grid=(bsz, C_CONV_CH // tc),
        in_specs=[pl.BlockSpec((s_len, tc), lambda b, j: (b, off + j)),
                  pl.BlockSpec((C_CONV, tc), lambda b, j: (0, j)),
                  pl.BlockSpec((1, tc), lambda b, j: (0, j))],
        out_specs=pl.BlockSpec((s_len, tc), lambda b, j: (b, j)),
        out_shape=jax.ShapeDtypeStruct((t, C_CONV_CH), F32),
        compiler_params=_cparams("parallel", "parallel"),
        name="conv_silu",
    )(proj, conv_w.astype(F32), conv_b.reshape(1, C_CONV_CH).astype(F32))


def _ssd_kernel(x_ref, b_ref, c_ref, z_ref, dt_ref, dtb_ref, alog_ref, dsk_ref, ng_ref, o_ref,
                st_sc, y_sc, *, nchunks):
    qn = SSD_CHUNK
    p = C_HEADDIM
    r = lax.broadcasted_iota(jnp.int32, (qn, qn), 0)
    cc = lax.broadcasted_iota(jnp.int32, (qn, qn), 1)
    causal = cc <= r
    t1 = jnp.where(causal, 1.0, 0.0)
    t2 = jnp.where(r > cc, 1.0, 0.0)
    a = -jnp.exp(alog_ref[0])
    dsk = dsk_ref[0]
    st_sc[...] = jnp.zeros(st_sc.shape, F32)

    def body(ci, carry):
        rows = pl.ds(pl.multiple_of(ci * qn, qn), qn)
        xc = x_ref[rows, :]
        bc = b_ref[rows, :]
        cm = c_ref[rows, :]
        dtr = dt_ref[0, 0, rows, :] + dtb_ref[0]
        dt = jnp.maximum(dtr, 0.0) + jnp.log(1.0 + jnp.exp(-jnp.abs(dtr)))
        dta = dt * a
        cmb = cm.astype(BF16)
        cb = _dot_nt(cmb, bc.astype(BF16))
        bt = bc.T.astype(BF16)
        heads = range(C_HPG)
        cols = [dta[:, h:h + 1] for h in heads]
        segs = [jnp.dot(t1, cols[h] * t2, precision=HIGHEST, preferred_element_type=F32) for h in heads]
        css = [segs[h][:, 0:1] + cols[h][0:1, :] for h in heads]
        xhs = [xc[:, h * p:(h + 1) * p] for h in heads]
        xdts = [xhs[h] * dt[:, h:h + 1] for h in heads]
        sts = [st_sc[h] for h in heads]
        mats = [(cb * jnp.where(causal, jnp.exp(segs[h]), 0.0)).astype(BF16) for h in heads]
        intra = [_dot(mats[h], xdts[h].astype(BF16)) for h in heads]
        inter = [_dot(cmb, sts[h].astype(BF16)) for h in heads]
        lasts = [css[h][qn - 1:qn, :] for h in heads]
        upd = [_dot(bt, (xdts[h] * jnp.exp(lasts[h] - css[h])).astype(BF16)) for h in heads]
        for h in heads:
            st_sc[h] = sts[h] * jnp.exp(lasts[h]) + upd[h]
            y_sc[:, h * p:(h + 1) * p] = (intra[h] + inter[h] * jnp.exp(css[h])
                                          + xhs[h] * dsk[:, h:h + 1])
        zz = z_ref[rows, :]
        y = y_sc[...] * (zz * jax.nn.sigmoid(zz))
        y = y * lax.rsqrt(jnp.mean(y * y, axis=-1, keepdims=True) + EPS) * ng_ref[...]
        o_ref[rows, :] = y.astype(o_ref.dtype)
        return carry

    lax.fori_loop(0, nchunks, body, 0)


def ssd_mixer(proj, xconv, dt_bias, a_log, d_skip, norm_g, bsz, s_len):
    t = bsz * s_len
    g = C_GROUPS
    gw = C_DINNER // g
    dtg = proj[:, CD_MISC:CD_MISC + C_HEADS].reshape(bsz, s_len, g, C_HPG).transpose(0, 2, 1, 3)
    per_group = lambda v: v.astype(F32).reshape(g, 1, C_HPG)
    nb = C_DINNER // LANES
    return pl.pallas_call(
        functools.partial(_ssd_kernel, nchunks=s_len // SSD_CHUNK),
        grid=(bsz, g),
        in_specs=[
            pl.BlockSpec((s_len, gw), lambda b, k: (b, k)),
            pl.BlockSpec((s_len, C_DSTATE), lambda b, k: (b, nb + k)),
            pl.BlockSpec((s_len, C_DSTATE), lambda b, k: (b, nb + g + k)),
            pl.BlockSpec((s_len, gw), lambda b, k: (b, k)),
            pl.BlockSpec((1, 1, s_len, C_HPG), lambda b, k: (b, k, 0, 0)),
            pl.BlockSpec((1, 1, C_HPG), lambda b, k: (k, 0, 0)),
            pl.BlockSpec((1, 1, C_HPG), lambda b, k: (k, 0, 0)),
            pl.BlockSpec((1, 1, C_HPG), lambda b, k: (k, 0, 0)),
            pl.BlockSpec((1, gw), lambda b, k: (0, k)),
        ],
        out_specs=pl.BlockSpec((s_len, gw), lambda b, k: (b, k)),
        out_shape=jax.ShapeDtypeStruct((t, C_DINNER), BF16),
        scratch_shapes=[pltpu.VMEM((C_HPG, C_DSTATE, C_HEADDIM), F32),
                        pltpu.VMEM((SSD_CHUNK, gw), F32)],
        compiler_params=_cparams("parallel", "parallel"),
        name="ssd",
    )(xconv, xconv, xconv, proj, dtg, per_group(dt_bias), per_group(a_log), per_group(d_skip),
      norm_g.reshape(1, C_DINNER).astype(F32))


def _dsa_kernel(far_ref, q_ref, qidx_ref, misc_ref, ckv_ref, kidx_ref, wuk_ref, wuv_ref, kvn_ref,
                band_ref, o_ref, ckvn_sc, kidx_sc, key_sc, lg_sc, mask_sc, *, widths, topk):
    tq = Q_BLOCK
    qi = pl.program_id(1)

    @pl.when(qi == 0)
    def _():
        c = ckv_ref[...]
        ckvn_sc[...] = (c * lax.rsqrt(jnp.mean(c * c, axis=-1, keepdims=True) + EPS)
                        * kvn_ref[...]).astype(BF16)
        kidx_sc[...] = kidx_ref[:, :IDX_DIM].astype(BF16)

    near = jnp.where(qi > 0, 1.0, 0.0)
    ur = lax.broadcasted_iota(jnp.int32, (LANES, LANES), 0)
    uc = lax.broadcasted_iota(jnp.int32, (LANES, LANES), 1)
    upper = jnp.where(ur <= uc, 1.0, 0.0).astype(BF16)
    kf = float(topk)
    int_min = jnp.int32(-2 ** 31)
    d0 = pl.ds(pl.multiple_of(qi * tq, tq), tq)
    d1 = pl.ds(pl.multiple_of(jnp.maximum(qi - 1, 0) * tq, tq), tq)

    def body(wd):
        qidx = qidx_ref[...].astype(BF16)
        w = misc_ref[:, MISC_WIDX:MISC_WIDX + IDX_HEADS] * ((IDX_HEADS * IDX_DIM) ** -0.5)
        kx = kidx_sc[:wd, :]
        sc = jnp.zeros((tq, wd), F32)
        for hi in range(IDX_HEADS):
            rel = _dot_nt(qidx[:, hi * IDX_DIM:(hi + 1) * IDX_DIM], kx)
            sc = sc + jnp.maximum(rel, 0.0) * w[:, hi:hi + 1]
        col = lax.broadcasted_iota(jnp.int32, (tq, wd), 1)
        row = lax.broadcasted_iota(jnp.int32, (tq, wd), 0) + qi * tq
        causal = col <= row
        sc = jnp.where(causal, sc, -jnp.inf)
        bits = pltpu.bitcast(sc, jnp.int32)
        key_sc[:, :wd] = jnp.where(bits < 0, bits ^ jnp.int32(0x7FFFFFFF), bits)

        def count_ge(cand):
            return jnp.sum(jnp.where(key_sc[:, :wd] >= cand, 1.0, 0.0), axis=-1, keepdims=True)

        zero = jnp.zeros((tq, 1), jnp.int32)
        prefix = jnp.where(count_ge(zero) >= kf, zero, zero + int_min)

        def bisect(i, prefix):
            cand = prefix | jnp.left_shift(jnp.int32(1), 30 - i)
            return jnp.where(count_ge(cand) >= kf, cand, prefix)

        thr = lax.fori_loop(0, 31, bisect, prefix)
        key = key_sc[:, :wd]
        gt = key > thr
        eq = key == thr
        need = kf - jnp.sum(jnp.where(gt, 1.0, 0.0), axis=-1, keepdims=True)
        eqf = jnp.where(eq, 1.0, 0.0)
        carry = jnp.zeros((tq, 1), F32)
        for j in range(wd // LANES):
            ls = slice(j * LANES, (j + 1) * LANES)
            e = eqf[:, ls]
            run = _dot(e.astype(BF16), upper) + carry
            carry = carry + jnp.sum(e, axis=-1, keepdims=True)
            take = jnp.where(gt[:, ls], 1.0, jnp.where(run <= need, e, 0.0))
            mask_sc[:, ls] = jnp.where(causal[:, ls], jnp.where(take > 0.5, 0.0, NEG), NEG)

        q = q_ref[...].astype(BF16)
        ck = ckvn_sc[:wd, :]
        for h in range(D_HEADS):
            qa = _dot(q[:, h * D_DH:(h + 1) * D_DH], wuk_ref[h])
            lg_sc[:, :wd] = _dot_nt(qa.astype(BF16), ck) * (D_DH ** -0.5) + far_ref[h]
            lg_sc[:, d0] += band_ref[h, 0]
            lg_sc[:, d1] += band_ref[h, 1] * near
            lg = lg_sc[:, :wd] + mask_sc[:, :wd]
            m = jnp.max(lg, axis=-1, keepdims=True)
            pr = jnp.exp(lg - m)
            ctx = _dot(pr.astype(BF16), ck) / jnp.sum(pr, axis=-1, keepdims=True)
            o_ref[:, h * D_DH:(h + 1) * D_DH] = _dot(ctx.astype(BF16), wuv_ref[h]).astype(o_ref.dtype)

    hi_key = (qi + 1) * tq
    lo = 0
    for wd in widths:
        @pl.when((hi_key > lo) & (hi_key <= wd))
        def _():
            body(wd)
        lo = wd


def _key_widths(s_len, tq, levels=4):
    nq = s_len // tq
    return tuple(sorted({-(-nq * (k + 1) // levels) * tq for k in range(levels)}))


def dsa_mixer(proj, kv_norm, w_uk, w_uv, rel_table, bsz, s_len):
    t = bsz * s_len
    tq = Q_BLOCK
    nq = s_len // tq
    topk = min(IDX_TOPK_MAX, s_len // 4)
    bias, causal = _near_bias(rel_table, tq)
    far = rel_table[REL_BUCKETS - 1].astype(F32)
    band = bias - far[:, None, None, None]
    band = band.at[:, 0].set(jnp.where(causal[None], band[:, 0], 0.0))
    hw = D_HEADS * D_DH
    return pl.pallas_call(
        functools.partial(_dsa_kernel, widths=_key_widths(s_len, tq), topk=topk),
        grid=(bsz, nq),
        in_specs=[
            pl.BlockSpec(memory_space=pltpu.SMEM),
            pl.BlockSpec((tq, hw), lambda b, i: (b * nq + i, CD_Q // hw)),
            pl.BlockSpec((tq, hw), lambda b, i: (b * nq + i, CD_QIDX // hw)),
            pl.BlockSpec((tq, LANES), lambda b, i: (b * nq + i, CD_MISC // LANES)),
            pl.BlockSpec((s_len, D_LATENT), lambda b, i: (b, CD_CKV // D_LATENT)),
            pl.BlockSpec((s_len, LANES), lambda b, i: (b, CD_KIDX // LANES)),
            pl.BlockSpec((D_HEADS, D_DH, D_LATENT), lambda b, i: (0, 0, 0)),
            pl.BlockSpec((D_HEADS, D_LATENT, D_DH), lambda b, i: (0, 0, 0)),
            pl.BlockSpec((1, D_LATENT), lambda b, i: (0, 0)),
            pl.BlockSpec((D_HEADS, 2, tq, tq), lambda b, i: (0, 0, 0, 0)),
        ],
        out_specs=pl.BlockSpec((tq, hw), lambda b, i: (b * nq + i, 0)),
        out_shape=jax.ShapeDtypeStruct((t, hw), BF16),
        scratch_shapes=[pltpu.VMEM((s_len, D_LATENT), BF16), pltpu.VMEM((s_len, IDX_DIM), BF16),
                        pltpu.VMEM((tq, s_len), jnp.int32), pltpu.VMEM((tq, s_len), F32),
                        pltpu.VMEM((tq, s_len), F32)],
        compiler_params=_cparams("parallel", "arbitrary"),
        name="dsa",
    )(far, proj, proj, proj, proj, proj, w_uk.astype(BF16), w_uv.astype(BF16),
      kv_norm.reshape(1, D_LATENT).astype(F32), band)


def _peer_tables():
    pairs = [(k1, k2) for k1 in range(PEER_TOPK) for k2 in range(PEER_TOPK)
             if (k1 + 1) * (k2 + 1) <= PEER_TOPK]
    n = PEER_NKEYS
    r1 = np.zeros((n, n), np.float32)
    r2 = np.zeros((n, n), np.float32)
    pad = np.full((n, 1), NEG, np.float32)
    for r, (k1, k2) in enumerate(pairs):
        r1[r, k1] = 1.0
        r2[r, k2] = 1.0
        pad[r, 0] = 0.0
    return jnp.asarray(r1), jnp.asarray(r2), jnp.asarray(pad), jnp.asarray(r1.T, dtype=BF16)


def _peer_select_kernel(q_ref, keys_ref, r1_ref, r2_ref, pad_ref, grp_ref,
                        rank2_ref, e2_ref, n_ref, coef_ref):
    tm = q_ref.shape[0]
    nk = PEER_NKEYS
    ridx = lax.broadcasted_iota(jnp.int32, (nk, tm), 0).astype(F32)
    kidx = lax.broadcasted_iota(jnp.int32, (PEER_TOPK, tm), 0).astype(F32)

    def extract(chains, track, by_index):
        ss = list(chains)
        ranks = [jnp.full((nk, tm), 99.0, F32) for _ in ss]
        vals = [jnp.zeros((PEER_TOPK, tm), F32) for _ in ss]
        for k in range(PEER_TOPK):
            ms = [jnp.max(s, axis=0, keepdims=True) for s in ss]
            if by_index:
                firsts = [jnp.min(jnp.where(s == m, ridx, 1e9), axis=0, keepdims=True)
                          for s, m in zip(ss, ms)]
                hits = [ridx == f for f in firsts]
            else:
                hits = [s == m for s, m in zip(ss, ms)]
            ss = [jnp.where(hit, -jnp.inf, s) for s, hit in zip(ss, hits)]
            if track:
                ranks = [jnp.where(hit, float(k), r) for r, hit in zip(ranks, hits)]
                vals = [jnp.where(kidx == float(k), m, v) for v, m in zip(vals, ms)]
        return ss, ranks, vals

    half = PEER_DQ // 2
    nh = q_ref.shape[1] // PEER_DQ
    pad_rows = jnp.zeros((nk - PEER_TOPK, tm), F32)
    scores = []
    for hh in range(nh):
        q = q_ref[:, hh * PEER_DQ:(hh + 1) * PEER_DQ].astype(BF16)
        scores.append(_dot_nt(keys_ref[hh, 0], q[:, :half]))
        scores.append(_dot_nt(keys_ref[hh, 1], q[:, half:]))
    def run(by_index):
        _, ranks, tops = extract(scores, True, by_index)
        cand0s = []
        for hh in range(nh):
            a1p = jnp.concatenate([tops[2 * hh], pad_rows], axis=0)
            a2p = jnp.concatenate([tops[2 * hh + 1], pad_rows], axis=0)
            cand0s.append(jnp.dot(r1_ref[...], a1p, precision=HIGHEST, preferred_element_type=F32)
                          + jnp.dot(r2_ref[...], a2p, precision=HIGHEST, preferred_element_type=F32)
                          + pad_ref[...])
        cands, _, _ = extract(cand0s, False, by_index)
        ties = jnp.zeros((1, tm), F32)
        for hh in range(nh):
            s1, s2 = scores[2 * hh], scores[2 * hh + 1]
            rank1, rank2 = ranks[2 * hh], ranks[2 * hh + 1]
            a1, a2 = tops[2 * hh], tops[2 * hh + 1]
            cand0, cand = cand0s[hh], cands[hh]
            taken = jnp.where((cand == -jnp.inf) & (cand0 > 0.5 * NEG), 1.0, 0.0)
            cnt = _dot(grp_ref[...], taken.astype(BF16))
            top = a1[0:1, :] + a2[0:1, :]
            zsum = jnp.sum(taken * jnp.exp(cand0 - top), axis=0, keepdims=True)
            n_i = jnp.zeros((nk, tm), F32)
            for k in range(PEER_TOPK):
                n_i = jnp.where(rank1 == float(k), cnt[k:k + 1, :], n_i)
            rank2_ref[hh] = rank2.astype(rank2_ref.dtype)
            e2_ref[hh] = jnp.exp(s2 - a2[0:1, :]).astype(e2_ref.dtype)
            n_ref[hh] = n_i
            coef_ref[hh] = jnp.exp(s1 - a1[0:1, :]) / zsum
            if not by_index:
                for removed in (jnp.where(rank1 < 99.0, 1.0, 0.0), jnp.where(rank2 < 99.0, 1.0, 0.0),
                                taken):
                    n_removed = jnp.sum(removed, axis=0, keepdims=True)
                    ties = ties + jnp.where(n_removed != float(PEER_TOPK), 1.0, 0.0)
        return ties

    ties = run(False)

    @pl.when(jnp.max(ties) > 0.0)
    def _():
        run(True)


def peer_select(q, keys, tm=128, hpb=2):
    t = q.shape[0]
    r1, r2, pad, grp = _peer_tables()
    nk = PEER_NKEYS
    full = lambda shape: pl.BlockSpec(shape, lambda i, h: (0,) * len(shape))
    out_spec = pl.BlockSpec((hpb, nk, tm), lambda i, h: (h, 0, i))
    sds = lambda dt: jax.ShapeDtypeStruct((PEER_HEADS, nk, t), dt)
    return pl.pallas_call(
        _peer_select_kernel,
        grid=(t // tm, PEER_HEADS // hpb),
        in_specs=[pl.BlockSpec((tm, hpb * PEER_DQ), lambda i, h: (i, h)),
                  pl.BlockSpec((hpb, 2, nk, PEER_DQ // 2), lambda i, h: (h, 0, 0, 0)),
                  full((nk, nk)), full((nk, nk)), full((nk, 1)), full((nk, nk))],
        out_specs=[out_spec] * 4,
        out_shape=[sds(BF16), sds(BF16), sds(F32), sds(F32)],
        compiler_params=_cparams("parallel", "parallel"),
        name="peer_select",
    )(q, keys.astype(BF16), r1, r2, pad, grp)


def _peer_dense_kernel(xnt_ref, u_ref, vt_ref, rank2_ref, e2_ref, n_ref, coef_ref, h_ref, o_ref,
                       acc_sc, y0_sc, y1_sc, z0_sc, z1_sc, *, et, tm):
    s = pl.program_id(1)
    nk = PEER_NKEYS

    @pl.when(s == 0)
    def _():
        acc_sc[...] = jnp.zeros(acc_sc.shape, F32)
        y1_sc[...] = jnp.zeros((et, tm), F32)
        z0_sc[...] = jnp.zeros((et, tm), BF16)

    def step(y_new, y_old, z_new, z_old):
        d = vt_ref.shape[0]

        def scores(p):
            cs = slice(p * (tm // 2), (p + 1) * (tm // 2))
            y_new[:, cs] = _dot(u_ref[...], xnt_ref[:, cs])

        def gated(r, lc):
            rs = slice(r * nk, (r + 1) * nk)
            ls = slice(lc * LANES, (lc + 1) * LANES)
            w = jnp.zeros((nk, LANES), BF16)
            for h in range(PEER_HEADS):
                nb = n_ref[h, 0, r:r + 1, ls].astype(BF16)
                cb = coef_ref[h, 0, r:r + 1, ls].astype(BF16)
                w = w + jnp.where(rank2_ref[h, :, ls] < nb, cb * e2_ref[h, :, ls],
                                  jnp.zeros((), BF16))
            yy = y_old[rs, ls]
            act = 0.5 * yy * (1.0 + jnp.tanh(0.7978845608028654 * (yy + 0.044715 * (yy * yy * yy))))
            z_new[rs, ls] = act.astype(BF16) * w

        def project(p):
            rows = slice(p * (d // 4), (p + 1) * (d // 4))
            acc_sc[rows, :] += _dot(vt_ref[rows, :], z_old[...])

        tiles = [(r, lc) for r in range(et // nk) for lc in range(tm // LANES)]
        pieces = [lambda: scores(0), lambda: project(0), lambda: project(1),
                  lambda: scores(1), lambda: project(2), lambda: project(3)]
        per = -(-len(tiles) // len(pieces))
        for k, piece in enumerate(pieces):
            piece()
            for r, lc in tiles[k * per:(k + 1) * per]:
                gated(r, lc)

    @pl.when(s % 2 == 0)
    def _():
        step(y0_sc, y1_sc, z1_sc, z0_sc)

    @pl.when(s % 2 == 1)
    def _():
        step(y1_sc, y0_sc, z0_sc, z1_sc)

    @pl.when(s == pl.num_programs(1) - 1)
    def _():
        o_ref[...] = h_ref[...] + acc_sc[...].T


def peer_dense(h, xnt, u, vt, rank2, e2, n_i, coef, tm=512, et=512):
    t, d = h.shape
    nblk = u.shape[0] // et
    ipb = et // PEER_NKEYS
    n4 = n_i.reshape(PEER_HEADS, PEER_NKEYS // ipb, ipb, t)
    c4 = coef.reshape(PEER_HEADS, PEER_NKEYS // ipb, ipb, t)
    last = nblk - 1
    blk = lambda s, lag: jnp.clip(s - lag, 0, last)
    tok = pl.BlockSpec((PEER_HEADS, PEER_NKEYS, tm), lambda i, s: (0, 0, i))
    per_i = pl.BlockSpec((PEER_HEADS, 1, ipb, tm), lambda i, s: (0, blk(s, 1), 0, i))
    return pl.pallas_call(
        functools.partial(_peer_dense_kernel, et=et, tm=tm),
        grid=(t // tm, nblk + 2),
        in_specs=[pl.BlockSpec((d, tm), lambda i, s: (0, i)),
                  pl.BlockSpec((et, d), lambda i, s: (blk(s, 0), 0)),
                  pl.BlockSpec((d, et), lambda i, s: (0, blk(s, 2))),
                  tok, tok, per_i, per_i,
                  pl.BlockSpec((tm, d), lambda i, s: (i, 0))],
        out_specs=pl.BlockSpec((tm, d), lambda i, s: (i, 0)),
        out_shape=jax.ShapeDtypeStruct((t, d), F32),
        scratch_shapes=[pltpu.VMEM((d, tm), F32), pltpu.VMEM((et, tm), F32), pltpu.VMEM((et, tm), F32),
                        pltpu.VMEM((et, tm), BF16), pltpu.VMEM((et, tm), BF16)],
        compiler_params=_cparams("parallel", "arbitrary"),
        name="peer_dense",
    )(xnt, u, vt, rank2, e2, n4, c4, h)


def peer_layer(h, norm_g, w_q, keys, u, v):
    q, xnt = matmul(h, w_q.astype(BF16), gain=norm_g, emit_xnt=True)
    rank2, e2, n_i, coef = peer_select(q, keys)
    return peer_dense(h, xnt, u.astype(BF16), v.T.astype(BF16), rank2, e2, n_i, coef)


def _rmsnorm_kernel(x_ref, g_ref, o_ref):
    x = x_ref[...]
    o_ref[...] = x * lax.rsqrt(jnp.mean(x * x, axis=-1, keepdims=True) + EPS) * g_ref[...]


def rmsnorm_final(x2d, g, tm=512):
    t, d = x2d.shape
    return pl.pallas_call(
        _rmsnorm_kernel,
        grid=(t // tm,),
        in_specs=[pl.BlockSpec((tm, d), lambda i: (i, 0)),
                  pl.BlockSpec((1, d), lambda i: (0, 0))],
        out_specs=pl.BlockSpec((tm, d), lambda i: (i, 0)),
        out_shape=jax.ShapeDtypeStruct((t, d), F32),
        compiler_params=_cparams("parallel"),
        name="rmsnorm_final",
    )(x2d, g.reshape(1, d).astype(F32))


def _ab_w_in(w):
    perm = np.concatenate([np.arange(0, B_DK, 2), np.arange(1, B_DK, 2)])
    start = A_HEADS * 2 * A_DH * 2 + A_HEADS * A_DV
    cols = np.arange(w.shape[1])
    for sec in range(2):
        for hd in range(B_HEADS):
            o = start + sec * B_HEADS * B_DK + hd * B_DK
            cols[o:o + B_DK] = o + perm
    return w[:, cols].astype(BF16)


def _cd_w_in(w):
    widths = (C_DINNER, C_CONV_CH, C_HEADS, D_HEADS * D_DH, D_LATENT, IDX_HEADS * IDX_DIM, IDX_DIM, IDX_HEADS)
    offs = np.concatenate([[0], np.cumsum(widths)])
    z, xbc, dt, q, ckv, qidx, kidx, widx = [w[:, offs[i]:offs[i + 1]] for i in range(8)]
    k = w.shape[0]
    pad = lambda n: jnp.zeros((k, n), w.dtype)
    out = jnp.concatenate([z, xbc, q, qidx, ckv, kidx, pad(LANES - IDX_DIM),
                           dt, widx, pad(LANES - C_HEADS - IDX_HEADS)], axis=1)
    assert out.shape[1] == CD_WIDTH
    return out.astype(BF16)


def kernel(x, rel_table, ab_w_in, ab_w_out, ab_lambda, ab_a_norm, ab_b_norm, cd_w_in, cd_w_out, cd_conv_w, cd_conv_b, cd_dt_bias, cd_a_log, cd_d_skip, cd_ssm_norm, cd_kv_norm, cd_w_uk, cd_w_uv, norm_mix, norm_ffn, peer_w_q, peer_keys, peer_u, peer_v, norm_final):
    bsz, s_len, d = x.shape
    h = x.reshape(bsz * s_len, d)
    for layer in range(DEPTH):
        i = layer // 2
        if layer % 2 == 0:
            proj = matmul(h, _ab_w_in(ab_w_in[i]), gain=norm_mix[layer])
            oa = diff_attention(proj, rel_table, ab_lambda[i], ab_a_norm[i], bsz, s_len, layer)
            ob = retention(proj, ab_b_norm[i], bsz, s_len)
            mixed = jnp.concatenate([oa, ob], axis=-1)
            h = matmul(mixed, ab_w_out[i].astype(BF16), residual=h)
        else:
            proj = matmul(h, _cd_w_in(cd_w_in[i]), gain=norm_mix[layer])
            xconv = conv_silu(proj, cd_conv_w[i], cd_conv_b[i], bsz, s_len)
            yc = ssd_mixer(proj, xconv, cd_dt_bias[i], cd_a_log[i], cd_d_skip[i], cd_ssm_norm[i],
                           bsz, s_len)
            yd = dsa_mixer(proj, cd_kv_norm[i], cd_w_uk[i], cd_w_uv[i], rel_table, bsz, s_len)
            mixed = jnp.concatenate([yc, yd], axis=-1)
            h = matmul(mixed, cd_w_out[i].astype(BF16), residual=h)
        h = peer_layer(h, norm_ffn[layer], peer_w_q[layer], peer_keys[layer], peer_u[layer],
                       peer_v[layer])
    return rmsnorm_final(h, norm_final).reshape(bsz, s_len, d)
```

```python
import functools
import math

import jax
import jax.numpy as jnp
import numpy as np
from jax import lax
from jax.experimental import pallas as pl
from jax.experimental.pallas import tpu as pltpu

D_MODEL = 2048
DEPTH = 2
EPS = 1e-6
Q_BLOCK = 128
REL_BUCKETS = 32
REL_MAX_DIST = 128
A_HEADS = 8
A_DH = 64
A_DV = 2 * A_DH
B_HEADS = 8
B_DK = 64
B_DV = 128
RET_CHUNK = 128
ROPE_BASE = 10000.0
C_DINNER = D_MODEL
C_HEADDIM = 64
C_HEADS = C_DINNER // C_HEADDIM
C_GROUPS = 4
C_HPG = C_HEADS // C_GROUPS
C_DSTATE = 128
C_CONV = 4
C_CONV_CH = C_DINNER + 2 * C_GROUPS * C_DSTATE
SSD_CHUNK = 128
D_HEADS = 8
D_DH = 128
D_LATENT = 256
IDX_HEADS = 16
IDX_DIM = 64
IDX_TOPK_MAX = 256
PEER_HEADS = 8
PEER_NKEYS = 128
PEER_DQ = 256
PEER_TOPK = 16

F32 = jnp.float32
BF16 = jnp.bfloat16
NEG = -1e30
LANES = 128
VMEM_LIMIT = 56 * 1024 * 1024
HIGHEST = lax.Precision.HIGHEST

CD_Z = 0
CD_XBC = CD_Z + C_DINNER
CD_Q = CD_XBC + C_CONV_CH
CD_QIDX = CD_Q + D_HEADS * D_DH
CD_CKV = CD_QIDX + IDX_HEADS * IDX_DIM
CD_KIDX = CD_CKV + D_LATENT
CD_MISC = CD_KIDX + LANES
CD_WIDTH = CD_MISC + LANES
MISC_WIDX = C_HEADS


def _dot(a, b):
    return jnp.dot(a, b, preferred_element_type=F32)


def _dot_nt(a, b):
    return lax.dot_general(a, b, (((1,), (1,)), ((), ())), preferred_element_type=F32)


def _cparams(*sem):
    return pltpu.CompilerParams(dimension_semantics=sem, vmem_limit_bytes=VMEM_LIMIT)


def rel_bucket(dist):
    n = jnp.maximum(dist, 0)
    max_exact = REL_BUCKETS // 2
    nf = jnp.maximum(n, 1).astype(F32)
    large = max_exact + (jnp.log(nf / max_exact) / math.log(REL_MAX_DIST / max_exact)
                         * (REL_BUCKETS - max_exact)).astype(jnp.int32)
    large = jnp.minimum(large, REL_BUCKETS - 1)
    return jnp.where(n < max_exact, n, large)


def _near_bias(rel_table, tq):
    assert tq >= REL_MAX_DIST
    r = jnp.arange(tq)[:, None]
    c = jnp.arange(tq)[None, :]
    d0 = r - c
    b0 = rel_table[rel_bucket(d0)].astype(F32)
    b1 = rel_table[rel_bucket(d0 + tq)].astype(F32)
    return jnp.stack([b0, b1], 0).transpose(3, 0, 1, 2), d0 >= 0


def _mm_kernel(*refs, norm, residual, emit_xnt):
    it = iter(refs)
    x_ref = next(it)
    g_ref = next(it) if norm else None
    w_ref = next(it)
    r_ref = next(it) if residual else None
    o_ref = next(it)
    xo_ref = next(it) if emit_xnt else None
    xn_ref = next(it)

    @pl.when(pl.program_id(1) == 0)
    def _():
        x = x_ref[...].astype(F32)
        if norm:
            x = x * lax.rsqrt(jnp.mean(x * x, axis=-1, keepdims=True) + EPS) * g_ref[...]
        xn_ref[...] = x.astype(BF16)
        if emit_xnt:
            xo_ref[...] = x.T.astype(BF16)

    acc = _dot(xn_ref[...], w_ref[...])
    if residual:
        acc = acc + r_ref[...]
    o_ref[...] = acc.astype(o_ref.dtype)


def matmul(x, w, *, gain=None, residual=None, emit_xnt=False, out_dtype=F32, tm=1024, tn=512):
    t, k = x.shape
    n = w.shape[1]
    tm = min(tm, t)
    assert t % tm == 0 and n % tn == 0 and w.shape[0] == k
    norm = gain is not None
    res = residual is not None
    in_specs = [pl.BlockSpec((tm, k), lambda i, j: (i, 0))]
    args = [x]
    if norm:
        in_specs.append(pl.BlockSpec((1, k), lambda i, j: (0, 0)))
        args.append(gain.reshape(1, k).astype(F32))
    in_specs.append(pl.BlockSpec((k, tn), lambda i, j: (0, j)))
    args.append(w)
    if res:
        in_specs.append(pl.BlockSpec((tm, tn), lambda i, j: (i, j)))
        args.append(residual)
    out_specs = [pl.BlockSpec((tm, tn), lambda i, j: (i, j))]
    out_shape = [jax.ShapeDtypeStruct((t, n), out_dtype)]
    if emit_xnt:
        out_specs.append(pl.BlockSpec((k, tm), lambda i, j: (0, i)))
        out_shape.append(jax.ShapeDtypeStruct((k, t), BF16))
    outs = pl.pallas_call(
        functools.partial(_mm_kernel, norm=norm, residual=res, emit_xnt=emit_xnt),
        grid=(t // tm, n // tn),
        in_specs=in_specs,
        out_specs=out_specs,
        out_shape=out_shape,
        scratch_shapes=[pltpu.VMEM((tm, k), BF16)],
        compiler_params=_cparams("parallel", "arbitrary"),
        name="matmul",
    )(*args)
    return outs if emit_xnt else outs[0]


def _diffattn_kernel(far_ref, q_ref, k_ref, v_ref, bias_ref, lam_ref, g_ref, o_ref,
                     m_sc, l_sc, acc_sc, *, tq, lam_init):
    h = pl.program_id(1)
    qi = pl.program_id(2)
    q = (q_ref[...] * (A_DH ** -0.5)).astype(BF16)
    qs = (q[:, :A_DH], q[:, A_DH:])
    m_sc[...] = jnp.full(m_sc.shape, NEG, F32)
    l_sc[...] = jnp.zeros(l_sc.shape, F32)
    acc_sc[...] = jnp.zeros(acc_sc.shape, F32)

    def process(j, bias):
        rows = pl.ds(pl.multiple_of(j * tq, tq), tq)
        kb = k_ref[rows, :].astype(BF16)
        vb = v_ref[rows, :].astype(BF16)
        for m in range(2):
            s = _dot_nt(qs[m], kb[:, m * A_DH:(m + 1) * A_DH]) + bias
            m_prev = m_sc[m]
            m_new = jnp.maximum(m_prev, jnp.max(s, axis=-1, keepdims=True))
            alpha = jnp.exp(m_prev - m_new)
            p = jnp.exp(s - m_new)
            l_sc[m] = alpha * l_sc[m] + jnp.sum(p, axis=-1, keepdims=True)
            acc_sc[m] = alpha * acc_sc[m] + _dot(p.astype(BF16), vb)
            m_sc[m] = m_new

    far = far_ref[h]

    def far_body(j, c):
        process(j, far)
        return c

    lax.fori_loop(0, jnp.maximum(qi - 1, 0), far_body, 0)

    @pl.when(qi > 0)
    def _():
        process(qi - 1, bias_ref[0, 1])

    process(qi, bias_ref[0, 0])

    lp = lam_ref[...]
    lam = (jnp.exp(jnp.sum(lp[0:1] * lp[1:2], keepdims=True))
           - jnp.exp(jnp.sum(lp[2:3] * lp[3:4], keepdims=True)) + lam_init)
    o = acc_sc[0] / l_sc[0] - lam * (acc_sc[1] / l_sc[1])
    o = o * lax.rsqrt(jnp.mean(o * o, axis=-1, keepdims=True) + EPS) * g_ref[...] * (1.0 - lam_init)
    o_ref[...] = o.astype(o_ref.dtype)


def diff_attention(proj, rel_table, lam_p, a_norm, bsz, s_len, layer, tq=512):
    t = bsz * s_len
    nq = s_len // tq
    lam_init = 0.8 - 0.6 * math.exp(-0.3 * layer)
    bias, causal = _near_bias(rel_table, tq)
    bias = bias.at[:, 0].set(jnp.where(causal[None], bias[:, 0], NEG))
    far = rel_table[REL_BUCKETS - 1].astype(F32)
    kcol = A_HEADS * 2 * A_DH // LANES
    return pl.pallas_call(
        functools.partial(_diffattn_kernel, tq=tq, lam_init=lam_init),
        grid=(bsz, A_HEADS, nq),
        in_specs=[
            pl.BlockSpec(memory_space=pltpu.SMEM),
            pl.BlockSpec((tq, LANES), lambda b, h, i: (b * nq + i, h)),
            pl.BlockSpec((s_len, LANES), lambda b, h, i: (b, kcol + h)),
            pl.BlockSpec((s_len, LANES), lambda b, h, i: (b, 2 * kcol + h)),
            pl.BlockSpec((1, 2, tq, tq), lambda b, h, i: (h, 0, 0, 0)),
            pl.BlockSpec((4, A_DH), lambda b, h, i: (0, 0)),
            pl.BlockSpec((1, A_DV), lambda b, h, i: (0, 0)),
        ],
        out_specs=pl.BlockSpec((tq, LANES), lambda b, h, i: (b * nq + i, h)),
        out_shape=jax.ShapeDtypeStruct((t, A_HEADS * A_DV), BF16),
        scratch_shapes=[pltpu.VMEM((2, tq, 1), F32), pltpu.VMEM((2, tq, 1), F32),
                        pltpu.VMEM((2, tq, A_DV), F32)],
        compiler_params=_cparams("parallel", "parallel", "arbitrary"),
        name="diff_attention",
    )(far, proj, proj, proj, bias, lam_p.astype(F32), a_norm.reshape(1, A_DV).astype(F32))


def _retention_kernel(q_ref, k_ref, v_ref, gate_ref, cos_ref, sin_ref, inner_ref, qdec_ref, kdec_ref,
                      cdec_ref, g_ref, o_ref, *, nchunks):
    c = RET_CHUNK
    lane = lax.broadcasted_iota(jnp.int32, (c, LANES), 1)
    even = (lane % 2) == 0

    def rope(x, cos, sin):
        partner = jnp.where(even, pltpu.roll(x, LANES - 1, axis=1), pltpu.roll(x, 1, axis=1))
        return x * cos + partner * sin

    def body(ci, states):
        rows = pl.ds(pl.multiple_of(ci * c, c), c)
        cos = cos_ref[rows, :]
        sin = sin_ref[rows, :]
        qr = rope(q_ref[rows, :], cos, sin) * (B_DK ** -0.5)
        kr = rope(k_ref[rows, :], cos, sin)
        qd = (qr * qdec_ref[0]).astype(BF16)
        kdt = (kr * kdec_ref[0]).T.astype(BF16)
        qb = qr.astype(BF16)
        kb = kr.astype(BF16)
        new_states = []
        for hh in range(2):
            sl = slice(hh * B_DK, (hh + 1) * B_DK)
            vs = slice(hh * B_DV, (hh + 1) * B_DV)
            vv = v_ref[rows, vs].astype(BF16)
            sc = _dot_nt(qb[:, sl], kb[:, sl]) * inner_ref[hh]
            o = _dot(sc.astype(BF16), vv) + _dot(qd[:, sl], states[hh].astype(BF16))
            new_states.append(states[hh] * cdec_ref[hh][0:1, :] + _dot(kdt[sl, :], vv))
            o = o * lax.rsqrt(jnp.mean(o * o, axis=-1, keepdims=True) + EPS) * g_ref[...]
            gt = gate_ref[rows, vs]
            o_ref[rows, vs] = (o * (gt * jax.nn.sigmoid(gt))).astype(o_ref.dtype)
        return tuple(new_states)

    zero = jnp.zeros((B_DK, B_DV), F32)
    lax.fori_loop(0, nchunks, body, (zero, zero))


def retention(proj, b_norm, bsz, s_len):
    t = bsz * s_len
    c = RET_CHUNK
    nh = B_HEADS
    log_gamma = jnp.log(1.0 - 2.0 ** (-5.0 - jnp.arange(nh, dtype=F32)))
    idx = jnp.arange(c, dtype=F32)
    rel = idx[:, None] - idx[None, :]
    inner = jnp.where(rel[None] >= 0, jnp.exp(rel[None] * log_gamma[:, None, None]), 0.0)
    q_decay = jnp.exp((idx[:, None] + 1.0) * log_gamma[None, :])
    k_decay = jnp.exp((c - 1.0 - idx[:, None]) * log_gamma[None, :])
    chunk_decay = jnp.exp(c * log_gamma)

    def pair_lanes(d):
        return jnp.repeat(d.T.reshape(nh // 2, 2, c).transpose(0, 2, 1), B_DK, axis=-1)

    cdec = jnp.broadcast_to(chunk_decay[:, None, None], (nh, 8, B_DV))
    inv = ROPE_BASE ** (-jnp.arange(0, B_DK, 2, dtype=F32) / B_DK)
    ang = jnp.arange(s_len, dtype=F32)[:, None] * inv[None, :]
    sign = jnp.tile(jnp.asarray([-1.0, 1.0], F32), B_DK // 2)
    cos = jnp.tile(jnp.repeat(jnp.cos(ang), 2, axis=-1), (1, 2))
    sin = jnp.tile(jnp.repeat(jnp.sin(ang), 2, axis=-1) * sign, (1, 2))
    base = (A_HEADS * 2 * A_DH * 2 + A_HEADS * A_DV) // LANES
    kblk = base + B_HEADS * B_DK // LANES
    vblk = (kblk + B_HEADS * B_DK // LANES) // 2
    gblk = vblk + B_HEADS * B_DV // (2 * LANES)
    return pl.pallas_call(
        functools.partial(_retention_kernel, nchunks=s_len // c),
        grid=(bsz, nh // 2),
        in_specs=[
            pl.BlockSpec((s_len, LANES), lambda b, p: (b, base + p)),
            pl.BlockSpec((s_len, LANES), lambda b, p: (b, kblk + p)),
            pl.BlockSpec((s_len, 2 * B_DV), lambda b, p: (b, vblk + p)),
            pl.BlockSpec((s_len, 2 * B_DV), lambda b, p: (b, gblk + p)),
            pl.BlockSpec((s_len, LANES), lambda b, p: (0, 0)),
            pl.BlockSpec((s_len, LANES), lambda b, p: (0, 0)),
            pl.BlockSpec((2, c, c), lambda b, p: (p, 0, 0)),
            pl.BlockSpec((1, c, LANES), lambda b, p: (p, 0, 0)),
            pl.BlockSpec((1, c, LANES), lambda b, p: (p, 0, 0)),
            pl.BlockSpec((2, 8, B_DV), lambda b, p: (p, 0, 0)),
            pl.BlockSpec((1, B_DV), lambda b, p: (0, 0)),
        ],
        out_specs=pl.BlockSpec((s_len, 2 * B_DV), lambda b, p: (b, p)),
        out_shape=jax.ShapeDtypeStruct((t, nh * B_DV), BF16),
        compiler_params=_cparams("parallel", "parallel"),
        name="retention",
    )(proj, proj, proj, proj, cos, sin, inner, pair_lanes(q_decay), pair_lanes(k_decay), cdec,
      b_norm.reshape(1, B_DV).astype(F32))


def _conv_kernel(x_ref, w_ref, b_ref, o_ref):
    x = x_ref[...]
    row = lax.broadcasted_iota(jnp.int32, x.shape, 0)
    acc = x * w_ref[C_CONV - 1:C_CONV, :] + b_ref[...]
    for j in range(1, C_CONV):
        xs = jnp.where(row >= j, pltpu.roll(x, j, axis=0), 0.0)
        acc = acc + xs * w_ref[C_CONV - 1 - j:C_CONV - j, :]
    o_ref[...] = acc * jax.nn.sigmoid(acc)


def conv_silu(proj, conv_w, conv_b, bsz, s_len, tc=512):
    t = bsz * s_len
    off = CD_XBC // tc
    return pl.pallas_call(
        _conv_kernel,
        grid=(bsz, C_CONV_CH // tc),
        in_specs=[pl.BlockSpec((s_len, tc), lambda b, j: (b, off + j)),
                  pl.BlockSpec((C_CONV, tc), lambda b, j: (0, j)),
                  pl.BlockSpec((1, tc), lambda b, j: (0, j))],
        out_specs=pl.BlockSpec((s_len, tc), lambda b, j: (b, j)),
        out_shape=jax.ShapeDtypeStruct((t, C_CONV_CH), F32),
        compiler_params=_cparams("parallel", "parallel"),
        name="conv_silu",
    )(proj, conv_w.astype(F32), conv_b.reshape(1, C_CONV_CH).astype(F32))


def _ssd_kernel(x_ref, b_ref, c_ref, z_ref, dt_ref, dtb_ref, alog_ref, dsk_ref, ng_ref, o_ref,
                st_sc, y_sc, *, nchunks):
    qn = SSD_CHUNK
    p = C_HEADDIM
    r = lax.broadcasted_iota(jnp.int32, (qn, qn), 0)
    cc = lax.broadcasted_iota(jnp.int32, (qn, qn), 1)
    causal = cc <= r
    t1 = jnp.where(causal, 1.0, 0.0)
    t2 = jnp.where(r > cc, 1.0, 0.0)
    a = -jnp.exp(alog_ref[0])
    dsk = dsk_ref[0]
    st_sc[...] = jnp.zeros(st_sc.shape, F32)

    def body(ci, carry):
        rows = pl.ds(pl.multiple_of(ci * qn, qn), qn)
        xc = x_ref[rows, :]
        bc = b_ref[rows, :]
        cm = c_ref[rows, :]
        dtr = dt_ref[0, 0, rows, :] + dtb_ref[0]
        dt = jnp.maximum(dtr, 0.0) + jnp.log(1.0 + jnp.exp(-jnp.abs(dtr)))
        dta = dt * a
        cmb = cm.astype(BF16)
        cb = _dot_nt(cmb, bc.astype(BF16))
        bt = bc.T.astype(BF16)
        heads = range(C_HPG)
        cols = [dta[:, h:h + 1] for h in heads]
        segs = [jnp.dot(t1, cols[h] * t2, precision=HIGHEST, preferred_element_type=F32) for h in heads]
        css = [segs[h][:, 0:1] + cols[h][0:1, :] for h in heads]
        xhs = [xc[:, h * p:(h + 1) * p] for h in heads]
        xdts = [xhs[h] * dt[:, h:h + 1] for h in heads]
        sts = [st_sc[h] for h in heads]
        mats = [(cb * jnp.where(causal, jnp.exp(segs[h]), 0.0)).astype(BF16) for h in heads]
        intra = [_dot(mats[h], xdts[h].astype(BF16)) for h in heads]
        inter = [_dot(cmb, sts[h].astype(BF16)) for h in heads]
        lasts = [css[h][qn - 1:qn, :] for h in heads]
        upd = [_dot(bt, (xdts[h] * jnp.exp(lasts[h] - css[h])).astype(BF16)) for h in heads]
        for h in heads:
            st_sc[h] = sts[h] * jnp.exp(lasts[h]) + upd[h]
            y_sc[:, h * p:(h + 1) * p] = (intra[h] + inter[h] * jnp.exp(css[h])
                                          + xhs[h] * dsk[:, h:h + 1])
        zz = z_ref[rows, :]
        y = y_sc[...] * (zz * jax.nn.sigmoid(zz))
        y = y * lax.rsqrt(jnp.mean(y * y, axis=-1, keepdims=True) + EPS) * ng_ref[...]
        o_ref[rows, :] = y.astype(o_ref.dtype)
        return carry

    lax.fori_loop(0, nchunks, body, 0)


def ssd_mixer(proj, xconv, dt_bias, a_log, d_skip, norm_g, bsz, s_len):
    t = bsz * s_len
    g = C_GROUPS
    gw = C_DINNER // g
    dtg = proj[:, CD_MISC:CD_MISC + C_HEADS].reshape(bsz, s_len, g, C_HPG).transpose(0, 2, 1, 3)
    per_group = lambda v: v.astype(F32).reshape(g, 1, C_HPG)
    nb = C_DINNER // LANES
    return pl.pallas_call(
        functools.partial(_ssd_kernel, nchunks=s_len // SSD_CHUNK),
        grid=(bsz, g),
        in_specs=[
            pl.BlockSpec((s_len, gw), lambda b, k: (b, k)),
            pl.BlockSpec((s_len, C_DSTATE), lambda b, k: (b, nb + k)),
            pl.BlockSpec((s_len, C_DSTATE), lambda b, k: (b, nb + g + k)),
            pl.BlockSpec((s_len, gw), lambda b, k: (b, k)),
            pl.BlockSpec((1, 1, s_len, C_HPG), lambda b, k: (b, k, 0, 0)),
            pl.BlockSpec((1, 1, C_HPG), lambda b, k: (k, 0, 0)),
            pl.BlockSpec((1, 1, C_HPG), lambda b, k: (k, 0, 0)),
            pl.BlockSpec((1, 1, C_HPG), lambda b, k: (k, 0, 0)),
            pl.BlockSpec((1, gw), lambda b, k: (0, k)),
        ],
        out_specs=pl.BlockSpec((s_len, gw), lambda b, k: (b, k)),
        out_shape=jax.ShapeDtypeStruct((t, C_DINNER), BF16),
        scratch_shapes=[pltpu.VMEM((C_HPG, C_DSTATE, C_HEADDIM), F32),
                        pltpu.VMEM((SSD_CHUNK, gw), F32)],
        compiler_params=_cparams("parallel", "parallel"),
        name="ssd",
    )(xconv, xconv, xconv, proj, dtg, per_group(dt_bias), per_group(a_log), per_group(d_skip),
      norm_g.reshape(1, C_DINNER).astype(F32))


def _dsa_kernel(far_ref, q_ref, qidx_ref, misc_ref, ckv_ref, kidx_ref, wuk_ref, wuv_ref, kvn_ref,
                band_ref, o_ref, ckvn_sc, kidx_sc, key_sc, lg_sc, mask_sc, *, widths, topk):
    tq = Q_BLOCK
    qi = pl.program_id(1)

    @pl.when(qi == 0)
    def _():
        c = ckv_ref[...]
        ckvn_sc[...] = (c * lax.rsqrt(jnp.mean(c * c, axis=-1, keepdims=True) + EPS)
                        * kvn_ref[...]).astype(BF16)
        kidx_sc[...] = kidx_ref[:, :IDX_DIM].astype(BF16)

    near = jnp.where(qi > 0, 1.0, 0.0)
    ur = lax.broadcasted_iota(jnp.int32, (LANES, LANES), 0)
    uc = lax.broadcasted_iota(jnp.int32, (LANES, LANES), 1)
    upper = jnp.where(ur <= uc, 1.0, 0.0).astype(BF16)
    kf = float(topk)
    int_min = jnp.int32(-2 ** 31)
    d0 = pl.ds(pl.multiple_of(qi * tq, tq), tq)
    d1 = pl.ds(pl.multiple_of(jnp.maximum(qi - 1, 0) * tq, tq), tq)

    def body(wd):
        qidx = qidx_ref[...].astype(BF16)
        w = misc_ref[:, MISC_WIDX:MISC_WIDX + IDX_HEADS] * ((IDX_HEADS * IDX_DIM) ** -0.5)
        kx = kidx_sc[:wd, :]
        sc = jnp.zeros((tq, wd), F32)
        for hi in range(IDX_HEADS):
            rel = _dot_nt(qidx[:, hi * IDX_DIM:(hi + 1) * IDX_DIM], kx)
            sc = sc + jnp.maximum(rel, 0.0) * w[:, hi:hi + 1]
        col = lax.broadcasted_iota(jnp.int32, (tq, wd), 1)
        row = lax.broadcasted_iota(jnp.int32, (tq, wd), 0) + qi * tq
        causal = col <= row
        sc = jnp.where(causal, sc, -jnp.inf)
        bits = pltpu.bitcast(sc, jnp.int32)
        key_sc[:, :wd] = jnp.where(bits < 0, bits ^ jnp.int32(0x7FFFFFFF), bits)

        def count_ge(cand):
            return jnp.sum(jnp.where(key_sc[:, :wd] >= cand, 1.0, 0.0), axis=-1, keepdims=True)

        zero = jnp.zeros((tq, 1), jnp.int32)
        prefix = jnp.where(count_ge(zero) >= kf, zero, zero + int_min)

        def bisect(i, prefix):
            cand = prefix | jnp.left_shift(jnp.int32(1), 30 - i)
            return jnp.where(count_ge(cand) >= kf, cand, prefix)

        thr = lax.fori_loop(0, 31, bisect, prefix)
        key = key_sc[:, :wd]
        gt = key > thr
        eq = key == thr
        need = kf - jnp.sum(jnp.where(gt, 1.0, 0.0), axis=-1, keepdims=True)
        eqf = jnp.where(eq, 1.0, 0.0)
        carry = jnp.zeros((tq, 1), F32)
        for j in range(wd // LANES):
            ls = slice(j * LANES, (j + 1) * LANES)
            e = eqf[:, ls]
            run = _dot(e.astype(BF16), upper) + carry
            carry = carry + jnp.sum(e, axis=-1, keepdims=True)
            take = jnp.where(gt[:, ls], 1.0, jnp.where(run <= need, e, 0.0))
            mask_sc[:, ls] = jnp.where(causal[:, ls], jnp.where(take > 0.5, 0.0, NEG), NEG)

        q = q_ref[...].astype(BF16)
        ck = ckvn_sc[:wd, :]
        for h in range(D_HEADS):
            qa = _dot(q[:, h * D_DH:(h + 1) * D_DH], wuk_ref[h])
            lg_sc[:, :wd] = _dot_nt(qa.astype(BF16), ck) * (D_DH ** -0.5) + far_ref[h]
            lg_sc[:, d0] += band_ref[h, 0]
            lg_sc[:, d1] += band_ref[h, 1] * near
            lg = lg_sc[:, :wd] + mask_sc[:, :wd]
            m = jnp.max(lg, axis=-1, keepdims=True)
            pr = jnp.exp(lg - m)
            ctx = _dot(pr.astype(BF16), ck) / jnp.sum(pr, axis=-1, keepdims=True)
            o_ref[:, h * D_DH:(h + 1) * D_DH] = _dot(ctx.astype(BF16), wuv_ref[h]).astype(o_ref.dtype)

    hi_key = (qi + 1) * tq
    lo = 0
    for wd in widths:
        @pl.when((hi_key > lo) & (hi_key <= wd))
        def _():
            body(wd)
        lo = wd


def _key_widths(s_len, tq, levels=4):
    nq = s_len // tq
    return tuple(sorted({-(-nq * (k + 1) // levels) * tq for k in range(levels)}))


def dsa_mixer(proj, kv_norm, w_uk, w_uv, rel_table, bsz, s_len):
    t = bsz * s_len
    tq = Q_BLOCK
    nq = s_len // tq
    topk = min(IDX_TOPK_MAX, s_len // 4)
    bias, causal = _near_bias(rel_table, tq)
    far = rel_table[REL_BUCKETS - 1].astype(F32)
    band = bias - far[:, None, None, None]
    band = band.at[:, 0].set(jnp.where(causal[None], band[:, 0], 0.0))
    hw = D_HEADS * D_DH
    return pl.pallas_call(
        functools.partial(_dsa_kernel, widths=_key_widths(s_len, tq), topk=topk),
        grid=(bsz, nq),
        in_specs=[
            pl.BlockSpec(memory_space=pltpu.SMEM),
            pl.BlockSpec((tq, hw), lambda b, i: (b * nq + i, CD_Q // hw)),
            pl.BlockSpec((tq, hw), lambda b, i: (b * nq + i, CD_QIDX // hw)),
            pl.BlockSpec((tq, LANES), lambda b, i: (b * nq + i, CD_MISC // LANES)),
            pl.BlockSpec((s_len, D_LATENT), lambda b, i: (b, CD_CKV // D_LATENT)),
            pl.BlockSpec((s_len, LANES), lambda b, i: (b, CD_KIDX // LANES)),
            pl.BlockSpec((D_HEADS, D_DH, D_LATENT), lambda b, i: (0, 0, 0)),
            pl.BlockSpec((D_HEADS, D_LATENT, D_DH), lambda b, i: (0, 0, 0)),
            pl.BlockSpec((1, D_LATENT), lambda b, i: (0, 0)),
            pl.BlockSpec((D_HEADS, 2, tq, tq), lambda b, i: (0, 0, 0, 0)),
        ],
        out_specs=pl.BlockSpec((tq, hw), lambda b, i: (b * nq + i, 0)),
        out_shape=jax.ShapeDtypeStruct((t, hw), BF16),
        scratch_shapes=[pltpu.VMEM((s_len, D_LATENT), BF16), pltpu.VMEM((s_len, IDX_DIM), BF16),
                        pltpu.VMEM((tq, s_len), jnp.int32), pltpu.VMEM((tq, s_len), F32),
                        pltpu.VMEM((tq, s_len), F32)],
        compiler_params=_cparams("parallel", "arbitrary"),
        name="dsa",
    )(far, proj, proj, proj, proj, proj, w_uk.astype(BF16), w_uv.astype(BF16),
      kv_norm.reshape(1, D_LATENT).astype(F32), band)


def _peer_tables():
    pairs = [(k1, k2) for k1 in range(PEER_TOPK) for k2 in range(PEER_TOPK)
             if (k1 + 1) * (k2 + 1) <= PEER_TOPK]
    n = PEER_NKEYS
    r1 = np.zeros((n, n), np.float32)
    r2 = np.zeros((n, n), np.float32)
    pad = np.full((n, 1), NEG, np.float32)
    for r, (k1, k2) in enumerate(pairs):
        r1[r, k1] = 1.0
        r2[r, k2] = 1.0
        pad[r, 0] = 0.0
    return jnp.asarray(r1), jnp.asarray(r2), jnp.asarray(pad), jnp.asarray(r1.T, dtype=BF16)


def _peer_select_kernel(q_ref, keys_ref, r1_ref, r2_ref, pad_ref, grp_ref,
                        rank2_ref, e2_ref, n_ref, coef_ref):
    tm = q_ref.shape[0]
    nk = PEER_NKEYS
    ridx = lax.broadcasted_iota(jnp.int32, (nk, tm), 0).astype(F32)
    kidx = lax.broadcasted_iota(jnp.int32, (PEER_TOPK, tm), 0).astype(F32)

    def extract(chains, track, by_index):
        ss = list(chains)
        ranks = [jnp.full((nk, tm), 99.0, F32) for _ in ss]
        vals = [jnp.zeros((PEER_TOPK, tm), F32) for _ in ss]
        for k in range(PEER_TOPK):
            ms = [jnp.max(s, axis=0, keepdims=True) for s in ss]
            if by_index:
                firsts = [jnp.min(jnp.where(s == m, ridx, 1e9), axis=0, keepdims=True)
                          for s, m in zip(ss, ms)]
                hits = [ridx == f for f in firsts]
            else:
                hits = [s == m for s, m in zip(ss, ms)]
            ss = [jnp.where(hit, -jnp.inf, s) for s, hit in zip(ss, hits)]
            if track:
                ranks = [jnp.where(hit, float(k), r) for r, hit in zip(ranks, hits)]
                vals = [jnp.where(kidx == float(k), m, v) for v, m in zip(vals, ms)]
        return ss, ranks, vals

    half = PEER_DQ // 2
    nh = q_ref.shape[1] // PEER_DQ
    pad_rows = jnp.zeros((nk - PEER_TOPK, tm), F32)
    scores = []
    for hh in range(nh):
        q = q_ref[:, hh * PEER_DQ:(hh + 1) * PEER_DQ].astype(BF16)
        scores.append(_dot_nt(keys_ref[hh, 0], q[:, :half]))
        scores.append(_dot_nt(keys_ref[hh, 1], q[:, half:]))
    def run(by_index):
        _, ranks, tops = extract(scores, True, by_index)
        cand0s = []
        for hh in range(nh):
            a1p = jnp.concatenate([tops[2 * hh], pad_rows], axis=0)
            a2p = jnp.concatenate([tops[2 * hh + 1], pad_rows], axis=0)
            cand0s.append(jnp.dot(r1_ref[...], a1p, precision=HIGHEST, preferred_element_type=F32)
                          + jnp.dot(r2_ref[...], a2p, precision=HIGHEST, preferred_element_type=F32)
                          + pad_ref[...])
        cands, _, _ = extract(cand0s, False, by_index)
        ties = jnp.zeros((1, tm), F32)
        for hh in range(nh):
            s1, s2 = scores[2 * hh], scores[2 * hh + 1]
            rank1, rank2 = ranks[2 * hh], ranks[2 * hh + 1]
            a1, a2 = tops[2 * hh], tops[2 * hh + 1]
            cand0, cand = cand0s[hh], cands[hh]
            taken = jnp.where((cand == -jnp.inf) & (cand0 > 0.5 * NEG), 1.0, 0.0)
            cnt = _dot(grp_ref[...], taken.astype(BF16))
            top = a1[0:1, :] + a2[0:1, :]
            zsum = jnp.sum(taken * jnp.exp(cand0 - top), axis=0, keepdims=True)
            n_i = jnp.zeros((nk, tm), F32)
            for k in range(PEER_TOPK):
                n_i = jnp.where(rank1 == float(k), cnt[k:k + 1, :], n_i)
            rank2_ref[hh] = rank2.astype(rank2_ref.dtype)
            e2_ref[hh] = jnp.exp(s2 - a2[0:1, :]).astype(e2_ref.dtype)
            n_ref[hh] = n_i
            coef_ref[hh] = jnp.exp(s1 - a1[0:1, :]) / zsum
            if not by_index:
                for removed in (jnp.where(rank1 < 99.0, 1.0, 0.0), jnp.where(rank2 < 99.0, 1.0, 0.0),
                                taken):
                    n_removed = jnp.sum(removed, axis=0, keepdims=True)
                    ties = ties + jnp.where(n_removed != float(PEER_TOPK), 1.0, 0.0)
        return ties

    ties = run(False)

    @pl.when(jnp.max(ties) > 0.0)
    def _():
        run(True)


def peer_select(q, keys, tm=128, hpb=2):
    t = q.shape[0]
    r1, r2, pad, grp = _peer_tables()
    nk = PEER_NKEYS
    full = lambda shape: pl.BlockSpec(shape, lambda i, h: (0,) * len(shape))
    out_spec = pl.BlockSpec((hpb, nk, tm), lambda i, h: (h, 0, i))
    sds = lambda dt: jax.ShapeDtypeStruct((PEER_HEADS, nk, t), dt)
    return pl.pallas_call(
        _peer_select_kernel,
        grid=(t // tm, PEER_HEADS // hpb),
        in_specs=[pl.BlockSpec((tm, hpb * PEER_DQ), lambda i, h: (i, h)),
                  pl.BlockSpec((hpb, 2, nk, PEER_DQ // 2), lambda i, h: (h, 0, 0, 0)),
                  full((nk, nk)), full((nk, nk)), full((nk, 1)), full((nk, nk))],
        out_specs=[out_spec] * 4,
        out_shape=[sds(BF16), sds(BF16), sds(F32), sds(F32)],
        compiler_params=_cparams("parallel", "parallel"),
        name="peer_select",
    )(q, keys.astype(BF16), r1, r2, pad, grp)


def _peer_dense_kernel(xnt_ref, u_ref, vt_ref, rank2_ref, e2_ref, n_ref, coef_ref, h_ref, o_ref,
                       acc_sc, y0_sc, y1_sc, z0_sc, z1_sc, *, et, tm):
    s = pl.program_id(1)
    nk = PEER_NKEYS
    nblk = pl.num_programs(1) - 2
    first_i = jnp.clip(s - 1, 0, nblk - 1) * (et // nk)
    zero, half, one = (jnp.asarray(c, BF16) for c in (0.0, 0.5, 1.0))
    gelu_c, gelu_a = jnp.asarray(0.7978845608028654, BF16), jnp.asarray(0.044715, BF16)
    ipb = et // nk
    assert ipb in (4, 8)
    group = pl.ds(pl.multiple_of((first_i // 8) * 8, 8), 8)
    upper_half = (first_i % 8) != 0

    def block_row(ref, h, r, ls):
        rows = ref[h, group, ls]
        if ipb == 8:
            return rows[r:r + 1]
        return jnp.where(upper_half, rows[ipb + r:ipb + r + 1], rows[r:r + 1])

    @pl.when(s == 0)
    def _():
        acc_sc[...] = jnp.zeros(acc_sc.shape, F32)
        y1_sc[...] = jnp.zeros((et, tm), F32)
        z0_sc[...] = jnp.zeros((et, tm), BF16)

    def step(y_new, y_old, z_new, z_old):
        d = vt_ref.shape[0]

        def scores(p):
            cs = slice(p * (tm // 2), (p + 1) * (tm // 2))
            y_new[:, cs] = _dot(u_ref[...], xnt_ref[:, cs])

        def gated(r, lc):
            rs = slice(r * nk, (r + 1) * nk)
            ls = slice(lc * LANES, (lc + 1) * LANES)
            w = jnp.zeros((nk, LANES), BF16)
            for h in range(PEER_HEADS):
                nb = block_row(n_ref, h, r, ls).astype(BF16)
                cb = block_row(coef_ref, h, r, ls).astype(BF16)
                gate = jnp.minimum(jnp.maximum(nb - rank2_ref[h, :, ls], zero), one)
                w = w + gate * (cb * e2_ref[h, :, ls])
            yy = y_old[rs, ls].astype(BF16)
            act = half * yy * (one + jnp.tanh(gelu_c * (yy + gelu_a * (yy * yy * yy))))
            z_new[rs, ls] = act * w

        def project(p):
            rows = slice(p * (d // 4), (p + 1) * (d // 4))
            acc_sc[rows, :] += _dot(vt_ref[rows, :], z_old[...])

        tiles = [(r, lc) for r in range(et // nk) for lc in range(tm // LANES)]
        pieces = [lambda: scores(0), lambda: project(0), lambda: project(1),
                  lambda: scores(1), lambda: project(2), lambda: project(3)]
        per = -(-len(tiles) // len(pieces))
        for k, piece in enumerate(pieces):
            piece()
            for r, lc in tiles[k * per:(k + 1) * per]:
                gated(r, lc)

    @pl.when(s % 2 == 0)
    def _():
        step(y0_sc, y1_sc, z1_sc, z0_sc)

    @pl.when(s % 2 == 1)
    def _():
        step(y1_sc, y0_sc, z0_sc, z1_sc)

    @pl.when(s == pl.num_programs(1) - 1)
    def _():
        o_ref[...] = h_ref[...] + acc_sc[...].T


def peer_dense(h, xnt, u, vt, rank2, e2, n_i, coef, tm=512, et=512):
    t, d = h.shape
    nblk = u.shape[0] // et
    last = nblk - 1
    blk = lambda s, lag: jnp.clip(s - lag, 0, last)
    tok = pl.BlockSpec((PEER_HEADS, PEER_NKEYS, tm), lambda i, s: (0, 0, i))
    return pl.pallas_call(
        functools.partial(_peer_dense_kernel, et=et, tm=tm),
        grid=(t // tm, nblk + 2),
        in_specs=[pl.BlockSpec((d, tm), lambda i, s: (0, i)),
                  pl.BlockSpec((et, d), lambda i, s: (blk(s, 0), 0)),
                  pl.BlockSpec((d, et), lambda i, s: (0, blk(s, 2))),
                  tok, tok, tok, tok,
                  pl.BlockSpec((tm, d), lambda i, s: (i, 0))],
        out_specs=pl.BlockSpec((tm, d), lambda i, s: (i, 0)),
        out_shape=jax.ShapeDtypeStruct((t, d), F32),
        scratch_shapes=[pltpu.VMEM((d, tm), F32), pltpu.VMEM((et, tm), F32), pltpu.VMEM((et, tm), F32),
                        pltpu.VMEM((et, tm), BF16), pltpu.VMEM((et, tm), BF16)],
        compiler_params=_cparams("parallel", "arbitrary"),
        name="peer_dense",
    )(xnt, u, vt, rank2, e2, n_i, coef, h)


def peer_layer(h, norm_g, w_q, keys, u, v):
    q, xnt = matmul(h, w_q.astype(BF16), gain=norm_g, emit_xnt=True)
    rank2, e2, n_i, coef = peer_select(q, keys)
    return peer_dense(h, xnt, u.astype(BF16), v.T.astype(BF16), rank2, e2, n_i, coef)


def _rmsnorm_kernel(x_ref, g_ref, o_ref):
    x = x_ref[...]
    o_ref[...] = x * lax.rsqrt(jnp.mean(x * x, axis=-1, keepdims=True) + EPS) * g_ref[...]


def rmsnorm_final(x2d, g, tm=512):
    t, d = x2d.shape
    return pl.pallas_call(
        _rmsnorm_kernel,
        grid=(t // tm,),
        in_specs=[pl.BlockSpec((tm, d), lambda i: (i, 0)),
                  pl.BlockSpec((1, d), lambda i: (0, 0))],
        out_specs=pl.BlockSpec((tm, d), lambda i: (i, 0)),
        out_shape=jax.ShapeDtypeStruct((t, d), F32),
        compiler_params=_cparams("parallel"),
        name="rmsnorm_final",
    )(x2d, g.reshape(1, d).astype(F32))


def _cd_w_in(w):
    widths = (C_DINNER, C_CONV_CH, C_HEADS, D_HEADS * D_DH, D_LATENT, IDX_HEADS * IDX_DIM, IDX_DIM, IDX_HEADS)
    offs = np.concatenate([[0], np.cumsum(widths)])
    z, xbc, dt, q, ckv, qidx, kidx, widx = [w[:, offs[i]:offs[i + 1]] for i in range(8)]
    k = w.shape[0]
    pad = lambda n: jnp.zeros((k, n), w.dtype)
    out = jnp.concatenate([z, xbc, q, qidx, ckv, kidx, pad(LANES - IDX_DIM),
                           dt, widx, pad(LANES - C_HEADS - IDX_HEADS)], axis=1)
    assert out.shape[1] == CD_WIDTH
    return out.astype(BF16)


def kernel(x, rel_table, ab_w_in, ab_w_out, ab_lambda, ab_a_norm, ab_b_norm, cd_w_in, cd_w_out, cd_conv_w, cd_conv_b, cd_dt_bias, cd_a_log, cd_d_skip, cd_ssm_norm, cd_kv_norm, cd_w_uk, cd_w_uv, norm_mix, norm_ffn, peer_w_q, peer_keys, peer_u, peer_v, norm_final):
    bsz, s_len, d = x.shape
    h = x.reshape(bsz * s_len, d)
    for layer in range(DEPTH):
        i = layer // 2
        if layer % 2 == 0:
            proj = matmul(h, ab_w_in[i].astype(BF16), gain=norm_mix[layer])
            oa = diff_attention(proj, rel_table, ab_lambda[i], ab_a_norm[i], bsz, s_len, layer)
            ob = retention(proj, ab_b_norm[i], bsz, s_len)
            mixed = jnp.concatenate([oa, ob], axis=-1)
            h = matmul(mixed, ab_w_out[i].astype(BF16), residual=h)
        else:
            proj = matmul(h, _cd_w_in(cd_w_in[i]), gain=norm_mix[layer])
            xconv = conv_silu(proj, cd_conv_w[i], cd_conv_b[i], bsz, s_len)
            yc = ssd_mixer(proj, xconv, cd_dt_bias[i], cd_a_log[i], cd_d_skip[i], cd_ssm_norm[i],
                           bsz, s_len)
            yd = dsa_mixer(proj, cd_kv_norm[i], cd_w_uk[i], cd_w_uv[i], rel_table, bsz, s_len)
            mixed = jnp.concatenate([yc, yd], axis=-1)
            h = matmul(mixed, cd_w_out[i].astype(BF16), residual=h)
        h = peer_layer(h, norm_ffn[layer], peer_w_q[layer], peer_keys[layer], peer_u[layer],
                       peer_v[layer])
    return rmsnorm_final(h, norm_final).reshape(bsz, s_len, d)
```

```python
import functools
import math

import jax
import jax.numpy as jnp
import numpy as np
from jax import lax
from jax.experimental import pallas as pl
from jax.experimental.pallas import tpu as pltpu

D_MODEL = 2048
DEPTH = 2
EPS = 1e-6
Q_BLOCK = 128
REL_BUCKETS = 32
REL_MAX_DIST = 128
A_HEADS = 8
A_DH = 64
A_DV = 2 * A_DH
B_HEADS = 8
B_DK = 64
B_DV = 128
RET_CHUNK = 128
ROPE_BASE = 10000.0
C_DINNER = D_MODEL
C_HEADDIM = 64
C_HEADS = C_DINNER // C_HEADDIM
C_GROUPS = 4
C_HPG = C_HEADS // C_GROUPS
C_DSTATE = 128
C_CONV = 4
C_CONV_CH = C_DINNER + 2 * C_GROUPS * C_DSTATE
SSD_CHUNK = 128
D_HEADS = 8
D_DH = 128
D_LATENT = 256
IDX_HEADS = 16
IDX_DIM = 64
IDX_TOPK_MAX = 256
PEER_HEADS = 8
PEER_NKEYS = 128
PEER_DQ = 256
PEER_TOPK = 16

F32 = jnp.float32
BF16 = jnp.bfloat16
NEG = -1e30
LANES = 128
VMEM_LIMIT = 56 * 1024 * 1024
HIGHEST = lax.Precision.HIGHEST

CD_Z = 0
CD_XBC = CD_Z + C_DINNER
CD_Q = CD_XBC + C_CONV_CH
CD_QIDX = CD_Q + D_HEADS * D_DH
CD_CKV = CD_QIDX + IDX_HEADS * IDX_DIM
CD_KIDX = CD_CKV + D_LATENT
CD_MISC = CD_KIDX + LANES
CD_WIDTH = CD_MISC + LANES
MISC_WIDX = C_HEADS


def _dot(a, b):
    return jnp.dot(a, b, preferred_element_type=F32)


def _dot_nt(a, b):
    return lax.dot_general(a, b, (((1,), (1,)), ((), ())), preferred_element_type=F32)


def _cparams(*sem):
    return pltpu.CompilerParams(dimension_semantics=sem, vmem_limit_bytes=VMEM_LIMIT)


def rel_bucket(dist):
    n = jnp.maximum(dist, 0)
    max_exact = REL_BUCKETS // 2
    nf = jnp.maximum(n, 1).astype(F32)
    large = max_exact + (jnp.log(nf / max_exact) / math.log(REL_MAX_DIST / max_exact)
                         * (REL_BUCKETS - max_exact)).astype(jnp.int32)
    large = jnp.minimum(large, REL_BUCKETS - 1)
    return jnp.where(n < max_exact, n, large)


def _near_bias(rel_table, tq):
    assert tq >= REL_MAX_DIST
    nh = rel_table.shape[1]
    dist = jnp.arange(-(tq - 1), 2 * tq)
    by_dist = rel_table[rel_bucket(dist)].astype(F32).T
    span = 3 * tq

    def toeplitz(k):
        ahead = by_dist[:, k - tq + 1:k + 1][:, ::-1]
        behind = by_dist[:, k + 1:k + tq][:, ::-1]
        row = jnp.concatenate([ahead, jnp.zeros((nh, span - 2 * tq + 1), F32), behind], axis=1)
        skew = jnp.tile(row, (1, tq))[:, :tq * (span - 1)].reshape(nh, tq, span - 1)
        return skew[:, :, :tq]

    r = jnp.arange(tq)[:, None]
    c = jnp.arange(tq)[None, :]
    return jnp.stack([toeplitz(tq - 1), toeplitz(2 * tq - 1)], axis=1), (r - c) >= 0


def _mm_kernel(*refs, norm, residual, emit_xnt):
    it = iter(refs)
    x_ref = next(it)
    g_ref = next(it) if norm else None
    w_ref = next(it)
    r_ref = next(it) if residual else None
    o_ref = next(it)
    xo_ref = next(it) if emit_xnt else None
    xn_ref = next(it)

    @pl.when(pl.program_id(1) == 0)
    def _():
        x = x_ref[...].astype(F32)
        if norm:
            x = x * lax.rsqrt(jnp.mean(x * x, axis=-1, keepdims=True) + EPS) * g_ref[...]
        xn_ref[...] = x.astype(BF16)
        if emit_xnt:
            xo_ref[...] = x.T.astype(BF16)

    acc = _dot(xn_ref[...], w_ref[...])
    if residual:
        acc = acc + r_ref[...]
    o_ref[...] = acc.astype(o_ref.dtype)


def matmul(x, w, *, gain=None, residual=None, emit_xnt=False, out_dtype=F32, tm=1024, tn=512):
    t, k = x.shape
    n = w.shape[1]
    tm = min(tm, t)
    assert t % tm == 0 and n % tn == 0 and w.shape[0] == k
    norm = gain is not None
    res = residual is not None
    in_specs = [pl.BlockSpec((tm, k), lambda i, j: (i, 0))]
    args = [x]
    if norm:
        in_specs.append(pl.BlockSpec((1, k), lambda i, j: (0, 0)))
        args.append(gain.reshape(1, k).astype(F32))
    in_specs.append(pl.BlockSpec((k, tn), lambda i, j: (0, j)))
    args.append(w)
    if res:
        in_specs.append(pl.BlockSpec((tm, tn), lambda i, j: (i, j)))
        args.append(residual)
    out_specs = [pl.BlockSpec((tm, tn), lambda i, j: (i, j))]
    out_shape = [jax.ShapeDtypeStruct((t, n), out_dtype)]
    if emit_xnt:
        out_specs.append(pl.BlockSpec((k, tm), lambda i, j: (0, i)))
        out_shape.append(jax.ShapeDtypeStruct((k, t), BF16))
    outs = pl.pallas_call(
        functools.partial(_mm_kernel, norm=norm, residual=res, emit_xnt=emit_xnt),
        grid=(t // tm, n // tn),
        in_specs=in_specs,
        out_specs=out_specs,
        out_shape=out_shape,
        scratch_shapes=[pltpu.VMEM((tm, k), BF16)],
        compiler_params=_cparams("parallel", "arbitrary"),
        name="matmul",
    )(*args)
    return outs if emit_xnt else outs[0]


def _diffattn_kernel(far_ref, q_ref, k_ref, v_ref, bias_ref, lam_ref, g_ref, o_ref,
                     m_sc, l_sc, acc_sc, *, tq, lam_init):
    h = pl.program_id(1)
    qi = pl.program_id(2)
    q = (q_ref[...] * (A_DH ** -0.5)).astype(BF16)
    qs = (q[:, :A_DH], q[:, A_DH:])
    m_sc[...] = jnp.full(m_sc.shape, NEG, F32)
    l_sc[...] = jnp.zeros(l_sc.shape, F32)
    acc_sc[...] = jnp.zeros(acc_sc.shape, F32)

    def process(j, bias):
        rows = pl.ds(pl.multiple_of(j * tq, tq), tq)
        kb = k_ref[rows, :].astype(BF16)
        vb = v_ref[rows, :].astype(BF16)
        for m in range(2):
            s = _dot_nt(qs[m], kb[:, m * A_DH:(m + 1) * A_DH]) + bias
            m_prev = m_sc[m]
            m_new = jnp.maximum(m_prev, jnp.max(s, axis=-1, keepdims=True))
            alpha = jnp.exp(m_prev - m_new)
            p = jnp.exp(s - m_new)
            l_sc[m] = alpha * l_sc[m] + jnp.sum(p, axis=-1, keepdims=True)
            acc_sc[m] = alpha * acc_sc[m] + _dot(p.astype(BF16), vb)
            m_sc[m] = m_new

    far = far_ref[h]

    def far_body(j, c):
        process(j, far)
        return c

    lax.fori_loop(0, jnp.maximum(qi - 1, 0), far_body, 0)

    @pl.when(qi > 0)
    def _():
        process(qi - 1, bias_ref[0, 1])

    process(qi, bias_ref[0, 0])

    lp = lam_ref[...]
    lam = (jnp.exp(jnp.sum(lp[0:1] * lp[1:2], keepdims=True))
           - jnp.exp(jnp.sum(lp[2:3] * lp[3:4], keepdims=True)) + lam_init)
    o = acc_sc[0] / l_sc[0] - lam * (acc_sc[1] / l_sc[1])
    o = o * lax.rsqrt(jnp.mean(o * o, axis=-1, keepdims=True) + EPS) * g_ref[...] * (1.0 - lam_init)
    o_ref[...] = o.astype(o_ref.dtype)


def diff_attention(proj, rel_table, lam_p, a_norm, bsz, s_len, layer, tq=512):
    t = bsz * s_len
    nq = s_len // tq
    lam_init = 0.8 - 0.6 * math.exp(-0.3 * layer)
    bias, causal = _near_bias(rel_table, tq)
    bias = bias.at[:, 0].set(jnp.where(causal[None], bias[:, 0], NEG))
    far = rel_table[REL_BUCKETS - 1].astype(F32)
    kcol = A_HEADS * 2 * A_DH // LANES
    return pl.pallas_call(
        functools.partial(_diffattn_kernel, tq=tq, lam_init=lam_init),
        grid=(bsz, A_HEADS, nq),
        in_specs=[
            pl.BlockSpec(memory_space=pltpu.SMEM),
            pl.BlockSpec((tq, LANES), lambda b, h, i: (b * nq + i, h)),
            pl.BlockSpec((s_len, LANES), lambda b, h, i: (b, kcol + h)),
            pl.BlockSpec((s_len, LANES), lambda b, h, i: (b, 2 * kcol + h)),
            pl.BlockSpec((1, 2, tq, tq), lambda b, h, i: (h, 0, 0, 0)),
            pl.BlockSpec((4, A_DH), lambda b, h, i: (0, 0)),
            pl.BlockSpec((1, A_DV), lambda b, h, i: (0, 0)),
        ],
        out_specs=pl.BlockSpec((tq, LANES), lambda b, h, i: (b * nq + i, h)),
        out_shape=jax.ShapeDtypeStruct((t, A_HEADS * A_DV), BF16),
        scratch_shapes=[pltpu.VMEM((2, tq, 1), F32), pltpu.VMEM((2, tq, 1), F32),
                        pltpu.VMEM((2, tq, A_DV), F32)],
        compiler_params=_cparams("parallel", "parallel", "arbitrary"),
        name="diff_attention",
    )(far, proj, proj, proj, bias, lam_p.astype(F32), a_norm.reshape(1, A_DV).astype(F32))


def _retention_kernel(q_ref, k_ref, v_ref, gate_ref, cos_ref, sin_ref, inner_ref, qdec_ref, kdec_ref,
                      cdec_ref, g_ref, o_ref, *, nchunks):
    c = RET_CHUNK
    lane = lax.broadcasted_iota(jnp.int32, (c, LANES), 1)
    even = (lane % 2) == 0

    def rope(x, cos, sin):
        partner = jnp.where(even, pltpu.roll(x, LANES - 1, axis=1), pltpu.roll(x, 1, axis=1))
        return x * cos + partner * sin

    def body(ci, states):
        rows = pl.ds(pl.multiple_of(ci * c, c), c)
        cos = cos_ref[rows, :]
        sin = sin_ref[rows, :]
        qr = rope(q_ref[rows, :], cos, sin) * (B_DK ** -0.5)
        kr = rope(k_ref[rows, :], cos, sin)
        qd = (qr * qdec_ref[0]).astype(BF16)
        kdt = (kr * kdec_ref[0]).T.astype(BF16)
        qb = qr.astype(BF16)
        kb = kr.astype(BF16)
        new_states = []
        for hh in range(2):
            sl = slice(hh * B_DK, (hh + 1) * B_DK)
            vs = slice(hh * B_DV, (hh + 1) * B_DV)
            vv = v_ref[rows, vs].astype(BF16)
            sc = _dot_nt(qb[:, sl], kb[:, sl]) * inner_ref[hh]
            o = _dot(sc.astype(BF16), vv) + _dot(qd[:, sl], states[hh].astype(BF16))
            new_states.append(states[hh] * cdec_ref[hh][0:1, :] + _dot(kdt[sl, :], vv))
            o = o * lax.rsqrt(jnp.mean(o * o, axis=-1, keepdims=True) + EPS) * g_ref[...]
            gt = gate_ref[rows, vs]
            o_ref[rows, vs] = (o * (gt * jax.nn.sigmoid(gt))).astype(o_ref.dtype)
        return tuple(new_states)

    zero = jnp.zeros((B_DK, B_DV), F32)
    lax.fori_loop(0, nchunks, body, (zero, zero))


def retention(proj, b_norm, bsz, s_len):
    t = bsz * s_len
    c = RET_CHUNK
    nh = B_HEADS
    log_gamma = jnp.log(1.0 - 2.0 ** (-5.0 - jnp.arange(nh, dtype=F32)))
    idx = jnp.arange(c, dtype=F32)
    rel = idx[:, None] - idx[None, :]
    inner = jnp.where(rel[None] >= 0, jnp.exp(rel[None] * log_gamma[:, None, None]), 0.0)
    q_decay = jnp.exp((idx[:, None] + 1.0) * log_gamma[None, :])
    k_decay = jnp.exp((c - 1.0 - idx[:, None]) * log_gamma[None, :])
    chunk_decay = jnp.exp(c * log_gamma)

    def pair_lanes(d):
        return jnp.repeat(d.T.reshape(nh // 2, 2, c).transpose(0, 2, 1), B_DK, axis=-1)

    cdec = jnp.broadcast_to(chunk_decay[:, None, None], (nh, 8, B_DV))
    inv = ROPE_BASE ** (-jnp.arange(0, B_DK, 2, dtype=F32) / B_DK)
    ang = jnp.arange(s_len, dtype=F32)[:, None] * inv[None, :]
    sign = jnp.tile(jnp.asarray([-1.0, 1.0], F32), B_DK // 2)
    cos = jnp.tile(jnp.repeat(jnp.cos(ang), 2, axis=-1), (1, 2))
    sin = jnp.tile(jnp.repeat(jnp.sin(ang), 2, axis=-1) * sign, (1, 2))
    base = (A_HEADS * 2 * A_DH * 2 + A_HEADS * A_DV) // LANES
    kblk = base + B_HEADS * B_DK // LANES
    vblk = (kblk + B_HEADS * B_DK // LANES) // 2
    gblk = vblk + B_HEADS * B_DV // (2 * LANES)
    return pl.pallas_call(
        functools.partial(_retention_kernel, nchunks=s_len // c),
        grid=(bsz, nh // 2),
        in_specs=[
            pl.BlockSpec((s_len, LANES), lambda b, p: (b, base + p)),
            pl.BlockSpec((s_len, LANES), lambda b, p: (b, kblk + p)),
            pl.BlockSpec((s_len, 2 * B_DV), lambda b, p: (b, vblk + p)),
            pl.BlockSpec((s_len, 2 * B_DV), lambda b, p: (b, gblk + p)),
            pl.BlockSpec((s_len, LANES), lambda b, p: (0, 0)),
            pl.BlockSpec((s_len, LANES), lambda b, p: (0, 0)),
            pl.BlockSpec((2, c, c), lambda b, p: (p, 0, 0)),
            pl.BlockSpec((1, c, LANES), lambda b, p: (p, 0, 0)),
            pl.BlockSpec((1, c, LANES), lambda b, p: (p, 0, 0)),
            pl.BlockSpec((2, 8, B_DV), lambda b, p: (p, 0, 0)),
            pl.BlockSpec((1, B_DV), lambda b, p: (0, 0)),
        ],
        out_specs=pl.BlockSpec((s_len, 2 * B_DV), lambda b, p: (b, p)),
        out_shape=jax.ShapeDtypeStruct((t, nh * B_DV), BF16),
        compiler_params=_cparams("parallel", "parallel"),
        name="retention",
    )(proj, proj, proj, proj, cos, sin, inner, pair_lanes(q_decay), pair_lanes(k_decay), cdec,
      b_norm.reshape(1, B_DV).astype(F32))


def _conv_kernel(x_ref, w_ref, b_ref, o_ref):
    x = x_ref[...]
    row = lax.broadcasted_iota(jnp.int32, x.shape, 0)
    acc = x * w_ref[C_CONV - 1:C_CONV, :] + b_ref[...]
    for j in range(1, C_CONV):
        xs = jnp.where(row >= j, pltpu.roll(x, j, axis=0), 0.0)
        acc = acc + xs * w_ref[C_CONV - 1 - j:C_CONV - j, :]
    o_ref[...] = acc * jax.nn.sigmoid(acc)


def conv_silu(proj, conv_w, conv_b, bsz, s_len, tc=512):
    t = bsz * s_len
    off = CD_XBC // tc
    return pl.pallas_call(
        _conv_kernel,
        grid=(bsz, C_CONV_CH // tc),
        in_specs=[pl.BlockSpec((s_len, tc), lambda b, j: (b, off + j)),
                  pl.BlockSpec((C_CONV, tc), lambda b, j: (0, j)),
                  pl.BlockSpec((1, tc), lambda b, j: (0, j))],
        out_specs=pl.BlockSpec((s_len, tc), lambda b, j: (b, j)),
        out_shape=jax.ShapeDtypeStruct((t, C_CONV_CH), F32),
        compiler_params=_cparams("parallel", "parallel"),
        name="conv_silu",
    )(proj, conv_w.astype(F32), conv_b.reshape(1, C_CONV_CH).astype(F32))


def _ssd_kernel(x_ref, b_ref, c_ref, z_ref, dt_ref, dtb_ref, alog_ref, dsk_ref, ng_ref, o_ref,
                st_sc, y_sc, *, nchunks):
    qn = SSD_CHUNK
    p = C_HEADDIM
    r = lax.broadcasted_iota(jnp.int32, (qn, qn), 0)
    cc = lax.broadcasted_iota(jnp.int32, (qn, qn), 1)
    causal = cc <= r
    t1 = jnp.where(causal, 1.0, 0.0)
    t2 = jnp.where(r > cc, 1.0, 0.0)
    a = -jnp.exp(alog_ref[0])
    dsk = dsk_ref[0]
    st_sc[...] = jnp.zeros(st_sc.shape, F32)

    def body(ci, carry):
        rows = pl.ds(pl.multiple_of(ci * qn, qn), qn)
        xc = x_ref[rows, :]
        bc = b_ref[rows, :]
        cm = c_ref[rows, :]
        dtr = dt_ref[0, 0, rows, :] + dtb_ref[0]
        dt = jnp.maximum(dtr, 0.0) + jnp.log(1.0 + jnp.exp(-jnp.abs(dtr)))
        dta = dt * a
        cmb = cm.astype(BF16)
        cb = _dot_nt(cmb, bc.astype(BF16))
        bt = bc.T.astype(BF16)
        heads = range(C_HPG)
        cols = [dta[:, h:h + 1] for h in heads]
        segs = [jnp.dot(t1, cols[h] * t2, precision=HIGHEST, preferred_element_type=F32) for h in heads]
        css = [segs[h][:, 0:1] + cols[h][0:1, :] for h in heads]
        xhs = [xc[:, h * p:(h + 1) * p] for h in heads]
        xdts = [xhs[h] * dt[:, h:h + 1] for h in heads]
        sts = [st_sc[h] for h in heads]
        mats = [(cb * jnp.where(causal, jnp.exp(segs[h]), 0.0)).astype(BF16) for h in heads]
        intra = [_dot(mats[h], xdts[h].astype(BF16)) for h in heads]
        inter = [_dot(cmb, sts[h].astype(BF16)) for h in heads]
        lasts = [css[h][qn - 1:qn, :] for h in heads]
        upd = [_dot(bt, (xdts[h] * jnp.exp(lasts[h] - css[h])).astype(BF16)) for h in heads]
        for h in heads:
            st_sc[h] = sts[h] * jnp.exp(lasts[h]) + upd[h]
            y_sc[:, h * p:(h + 1) * p] = (intra[h] + inter[h] * jnp.exp(css[h])
                                          + xhs[h] * dsk[:, h:h + 1])
        zz = z_ref[rows, :]
        y = y_sc[...] * (zz * jax.nn.sigmoid(zz))
        y = y * lax.rsqrt(jnp.mean(y * y, axis=-1, keepdims=True) + EPS) * ng_ref[...]
        o_ref[rows, :] = y.astype(o_ref.dtype)
        return carry

    lax.fori_loop(0, nchunks, body, 0)


def ssd_mixer(proj, xconv, dt_bias, a_log, d_skip, norm_g, bsz, s_len):
    t = bsz * s_len
    g = C_GROUPS
    gw = C_DINNER // g
    dtg = proj[:, CD_MISC:CD_MISC + C_HEADS].reshape(bsz, s_len, g, C_HPG).transpose(0, 2, 1, 3)
    per_group = lambda v: v.astype(F32).reshape(g, 1, C_HPG)
    nb = C_DINNER // LANES
    return pl.pallas_call(
        functools.partial(_ssd_kernel, nchunks=s_len // SSD_CHUNK),
        grid=(bsz, g),
        in_specs=[
            pl.BlockSpec((s_len, gw), lambda b, k: (b, k)),
            pl.BlockSpec((s_len, C_DSTATE), lambda b, k: (b, nb + k)),
            pl.BlockSpec((s_len, C_DSTATE), lambda b, k: (b, nb + g + k)),
            pl.BlockSpec((s_len, gw), lambda b, k: (b, k)),
            pl.BlockSpec((1, 1, s_len, C_HPG), lambda b, k: (b, k, 0, 0)),
            pl.BlockSpec((1, 1, C_HPG), lambda b, k: (k, 0, 0)),
            pl.BlockSpec((1, 1, C_HPG), lambda b, k: (k, 0, 0)),
            pl.BlockSpec((1, 1, C_HPG), lambda b, k: (k, 0, 0)),
            pl.BlockSpec((1, gw), lambda b, k: (0, k)),
        ],
        out_specs=pl.BlockSpec((s_len, gw), lambda b, k: (b, k)),
        out_shape=jax.ShapeDtypeStruct((t, C_DINNER), BF16),
        scratch_shapes=[pltpu.VMEM((C_HPG, C_DSTATE, C_HEADDIM), F32),
                        pltpu.VMEM((SSD_CHUNK, gw), F32)],
        compiler_params=_cparams("parallel", "parallel"),
        name="ssd",
    )(xconv, xconv, xconv, proj, dtg, per_group(dt_bias), per_group(a_log), per_group(d_skip),
      norm_g.reshape(1, C_DINNER).astype(F32))


def _dsa_kernel(far_ref, q_ref, qidx_ref, misc_ref, ckv_ref, kidx_ref, wuk_ref, wuv_ref, kvn_ref,
                band_ref, o_ref, ckvn_sc, kidx_sc, key_sc, lg_sc, mask_sc, *, widths, topk):
    tq = Q_BLOCK
    qi = pl.program_id(1)

    @pl.when(qi == 0)
    def _():
        c = ckv_ref[...]
        ckvn_sc[...] = (c * lax.rsqrt(jnp.mean(c * c, axis=-1, keepdims=True) + EPS)
                        * kvn_ref[...]).astype(BF16)
        kidx_sc[...] = kidx_ref[:, :IDX_DIM].astype(BF16)

    near = jnp.where(qi > 0, 1.0, 0.0)
    ur = lax.broadcasted_iota(jnp.int32, (LANES, LANES), 0)
    uc = lax.broadcasted_iota(jnp.int32, (LANES, LANES), 1)
    upper = jnp.where(ur <= uc, 1.0, 0.0).astype(BF16)
    kf = float(topk)
    int_min = jnp.int32(-2 ** 31)
    d0 = pl.ds(pl.multiple_of(qi * tq, tq), tq)
    d1 = pl.ds(pl.multiple_of(jnp.maximum(qi - 1, 0) * tq, tq), tq)

    def body(wd):
        qidx = qidx_ref[...].astype(BF16)
        w = misc_ref[:, MISC_WIDX:MISC_WIDX + IDX_HEADS] * ((IDX_HEADS * IDX_DIM) ** -0.5)
        kx = kidx_sc[:wd, :]
        sc = jnp.zeros((tq, wd), F32)
        for hi in range(IDX_HEADS):
            rel = _dot_nt(qidx[:, hi * IDX_DIM:(hi + 1) * IDX_DIM], kx)
            sc = sc + jnp.maximum(rel, 0.0) * w[:, hi:hi + 1]
        col = lax.broadcasted_iota(jnp.int32, (tq, wd), 1)
        row = lax.broadcasted_iota(jnp.int32, (tq, wd), 0) + qi * tq
        causal = col <= row
        sc = jnp.where(causal, sc, -jnp.inf)
        bits = pltpu.bitcast(sc, jnp.int32)
        key_sc[:, :wd] = jnp.where(bits < 0, bits ^ jnp.int32(0x7FFFFFFF), bits)

        def count_ge(cand):
            return jnp.sum(jnp.where(key_sc[:, :wd] >= cand, 1.0, 0.0), axis=-1, keepdims=True)

        zero = jnp.zeros((tq, 1), jnp.int32)
        prefix = jnp.where(count_ge(zero) >= kf, zero, zero + int_min)

        def bisect(i, prefix):
            cand = prefix | jnp.left_shift(jnp.int32(1), 30 - i)
            return jnp.where(count_ge(cand) >= kf, cand, prefix)

        thr = lax.fori_loop(0, 31, bisect, prefix)
        key = key_sc[:, :wd]
        gt = key > thr
        eq = key == thr
        need = kf - jnp.sum(jnp.where(gt, 1.0, 0.0), axis=-1, keepdims=True)
        eqf = jnp.where(eq, 1.0, 0.0)
        carry = jnp.zeros((tq, 1), F32)
        for j in range(wd // LANES):
            ls = slice(j * LANES, (j + 1) * LANES)
            e = eqf[:, ls]
            run = _dot(e.astype(BF16), upper) + carry
            carry = carry + jnp.sum(e, axis=-1, keepdims=True)
            take = jnp.where(gt[:, ls], 1.0, jnp.where(run <= need, e, 0.0))
            mask_sc[:, ls] = jnp.where(causal[:, ls], jnp.where(take > 0.5, 0.0, NEG), NEG)

        q = q_ref[...].astype(BF16)
        ck = ckvn_sc[:wd, :]
        for h in range(D_HEADS):
            qa = _dot(q[:, h * D_DH:(h + 1) * D_DH], wuk_ref[h])
            lg_sc[:, :wd] = _dot_nt(qa.astype(BF16), ck) * (D_DH ** -0.5) + far_ref[h]
            lg_sc[:, d0] += band_ref[h, 0]
            lg_sc[:, d1] += band_ref[h, 1] * near
            lg = lg_sc[:, :wd] + mask_sc[:, :wd]
            m = jnp.max(lg, axis=-1, keepdims=True)
            pr = jnp.exp(lg - m)
            ctx = _dot(pr.astype(BF16), ck) / jnp.sum(pr, axis=-1, keepdims=True)
            o_ref[:, h * D_DH:(h + 1) * D_DH] = _dot(ctx.astype(BF16), wuv_ref[h]).astype(o_ref.dtype)

    hi_key = (qi + 1) * tq
    lo = 0
    for wd in widths:
        @pl.when((hi_key > lo) & (hi_key <= wd))
        def _():
            body(wd)
        lo = wd


def _key_widths(s_len, tq, levels=4):
    nq = s_len // tq
    return tuple(sorted({-(-nq * (k + 1) // levels) * tq for k in range(levels)}))


def dsa_mixer(proj, kv_norm, w_uk, w_uv, rel_table, bsz, s_len):
    t = bsz * s_len
    tq = Q_BLOCK
    nq = s_len // tq
    topk = min(IDX_TOPK_MAX, s_len // 4)
    bias, causal = _near_bias(rel_table, tq)
    far = rel_table[REL_BUCKETS - 1].astype(F32)
    band = bias - far[:, None, None, None]
    band = band.at[:, 0].set(jnp.where(causal[None], band[:, 0], 0.0))
    hw = D_HEADS * D_DH
    return pl.pallas_call(
        functools.partial(_dsa_kernel, widths=_key_widths(s_len, tq), topk=topk),
        grid=(bsz, nq),
        in_specs=[
            pl.BlockSpec(memory_space=pltpu.SMEM),
            pl.BlockSpec((tq, hw), lambda b, i: (b * nq + i, CD_Q // hw)),
            pl.BlockSpec((tq, hw), lambda b, i: (b * nq + i, CD_QIDX // hw)),
            pl.BlockSpec((tq, LANES), lambda b, i: (b * nq + i, CD_MISC // LANES)),
            pl.BlockSpec((s_len, D_LATENT), lambda b, i: (b, CD_CKV // D_LATENT)),
            pl.BlockSpec((s_len, LANES), lambda b, i: (b, CD_KIDX // LANES)),
            pl.BlockSpec((D_HEADS, D_DH, D_LATENT), lambda b, i: (0, 0, 0)),
            pl.BlockSpec((D_HEADS, D_LATENT, D_DH), lambda b, i: (0, 0, 0)),
            pl.BlockSpec((1, D_LATENT), lambda b, i: (0, 0)),
            pl.BlockSpec((D_HEADS, 2, tq, tq), lambda b, i: (0, 0, 0, 0)),
        ],
        out_specs=pl.BlockSpec((tq, hw), lambda b, i: (b * nq + i, 0)),
        out_shape=jax.ShapeDtypeStruct((t, hw), BF16),
        scratch_shapes=[pltpu.VMEM((s_len, D_LATENT), BF16), pltpu.VMEM((s_len, IDX_DIM), BF16),
                        pltpu.VMEM((tq, s_len), jnp.int32), pltpu.VMEM((tq, s_len), F32),
                        pltpu.VMEM((tq, s_len), F32)],
        compiler_params=_cparams("parallel", "arbitrary"),
        name="dsa",
    )(far, proj, proj, proj, proj, proj, w_uk.astype(BF16), w_uv.astype(BF16),
      kv_norm.reshape(1, D_LATENT).astype(F32), band)


def _peer_tables():
    pairs = [(k1, k2) for k1 in range(PEER_TOPK) for k2 in range(PEER_TOPK)
             if (k1 + 1) * (k2 + 1) <= PEER_TOPK]
    n = PEER_NKEYS
    r1 = np.zeros((n, n), np.float32)
    r2 = np.zeros((n, n), np.float32)
    pad = np.full((n, 1), NEG, np.float32)
    for r, (k1, k2) in enumerate(pairs):
        r1[r, k1] = 1.0
        r2[r, k2] = 1.0
        pad[r, 0] = 0.0
    return jnp.asarray(r1), jnp.asarray(r2), jnp.asarray(pad), jnp.asarray(r1.T, dtype=BF16)


def _peer_select_kernel(q_ref, keys_ref, r1_ref, r2_ref, pad_ref, grp_ref,
                        rank2_ref, e2_ref, n_ref, coef_ref):
    tm = q_ref.shape[0]
    nk = PEER_NKEYS
    ridx = lax.broadcasted_iota(jnp.int32, (nk, tm), 0).astype(F32)
    kidx = lax.broadcasted_iota(jnp.int32, (PEER_TOPK, tm), 0).astype(F32)

    def extract(chains, track, by_index):
        ss = list(chains)
        ranks = [jnp.full((nk, tm), 99.0, F32) for _ in ss]
        vals = [jnp.zeros((PEER_TOPK, tm), F32) for _ in ss]
        for k in range(PEER_TOPK):
            ms = [jnp.max(s, axis=0, keepdims=True) for s in ss]
            if by_index:
                firsts = [jnp.min(jnp.where(s == m, ridx, 1e9), axis=0, keepdims=True)
                          for s, m in zip(ss, ms)]
                hits = [ridx == f for f in firsts]
            else:
                hits = [s == m for s, m in zip(ss, ms)]
            ss = [jnp.where(hit, -jnp.inf, s) for s, hit in zip(ss, hits)]
            if track:
                ranks = [jnp.where(hit, float(k), r) for r, hit in zip(ranks, hits)]
                vals = [jnp.where(kidx == float(k), m, v) for v, m in zip(vals, ms)]
        return ss, ranks, vals

    half = PEER_DQ // 2
    nh = q_ref.shape[1] // PEER_DQ
    pad_rows = jnp.zeros((nk - PEER_TOPK, tm), F32)
    scores = []
    for hh in range(nh):
        q = q_ref[:, hh * PEER_DQ:(hh + 1) * PEER_DQ].astype(BF16)
        scores.append(_dot_nt(keys_ref[hh, 0], q[:, :half]))
        scores.append(_dot_nt(keys_ref[hh, 1], q[:, half:]))
    def run(by_index):
        _, ranks, tops = extract(scores, True, by_index)
        cand0s = []
        for hh in range(nh):
            a1p = jnp.concatenate([tops[2 * hh], pad_rows], axis=0)
            a2p = jnp.concatenate([tops[2 * hh + 1], pad_rows], axis=0)
            cand0s.append(jnp.dot(r1_ref[...], a1p, precision=HIGHEST, preferred_element_type=F32)
                          + jnp.dot(r2_ref[...], a2p, precision=HIGHEST, preferred_element_type=F32)
                          + pad_ref[...])
        cands, _, _ = extract(cand0s, False, by_index)
        ties = jnp.zeros((1, tm), F32)
        for hh in range(nh):
            s1, s2 = scores[2 * hh], scores[2 * hh + 1]
            rank1, rank2 = ranks[2 * hh], ranks[2 * hh + 1]
            a1, a2 = tops[2 * hh], tops[2 * hh + 1]
            cand0, cand = cand0s[hh], cands[hh]
            taken = jnp.where((cand == -jnp.inf) & (cand0 > 0.5 * NEG), 1.0, 0.0)
            cnt = _dot(grp_ref[...], taken.astype(BF16))
            top = a1[0:1, :] + a2[0:1, :]
            zsum = jnp.sum(taken * jnp.exp(cand0 - top), axis=0, keepdims=True)
            n_i = jnp.zeros((nk, tm), F32)
            for k in range(PEER_TOPK):
                n_i = jnp.where(rank1 == float(k), cnt[k:k + 1, :], n_i)
            rank2_ref[hh] = rank2.astype(rank2_ref.dtype)
            e2_ref[hh] = jnp.exp(s2 - a2[0:1, :]).astype(e2_ref.dtype)
            n_ref[hh] = n_i
            coef_ref[hh] = jnp.exp(s1 - a1[0:1, :]) / zsum
            if not by_index:
                for removed in (jnp.where(rank1 < 99.0, 1.0, 0.0), jnp.where(rank2 < 99.0, 1.0, 0.0),
                                taken):
                    n_removed = jnp.sum(removed, axis=0, keepdims=True)
                    ties = ties + jnp.where(n_removed != float(PEER_TOPK), 1.0, 0.0)
        return ties

    ties = run(False)

    @pl.when(jnp.max(ties) > 0.0)
    def _():
        run(True)


def peer_select(q, keys, tm=128, hpb=2):
    t = q.shape[0]
    r1, r2, pad, grp = _peer_tables()
    nk = PEER_NKEYS
    full = lambda shape: pl.BlockSpec(shape, lambda i, h: (0,) * len(shape))
    out_spec = pl.BlockSpec((hpb, nk, tm), lambda i, h: (h, 0, i))
    sds = lambda dt: jax.ShapeDtypeStruct((PEER_HEADS, nk, t), dt)
    return pl.pallas_call(
        _peer_select_kernel,
        grid=(t // tm, PEER_HEADS // hpb),
        in_specs=[pl.BlockSpec((tm, hpb * PEER_DQ), lambda i, h: (i, h)),
                  pl.BlockSpec((hpb, 2, nk, PEER_DQ // 2), lambda i, h: (h, 0, 0, 0)),
                  full((nk, nk)), full((nk, nk)), full((nk, 1)), full((nk, nk))],
        out_specs=[out_spec] * 4,
        out_shape=[sds(BF16), sds(BF16), sds(F32), sds(F32)],
        compiler_params=_cparams("parallel", "parallel"),
        name="peer_select",
    )(q, keys.astype(BF16), r1, r2, pad, grp)


def _peer_dense_kernel(xnt_ref, u_ref, vt_ref, rank2_ref, e2_ref, n_ref, coef_ref, h_ref, o_ref,
                       acc_sc, y0_sc, y1_sc, z0_sc, z1_sc, *, et, tm):
    s = pl.program_id(1)
    nk = PEER_NKEYS
    nblk = pl.num_programs(1) - 2
    first_i = jnp.clip(s - 1, 0, nblk - 1) * (et // nk)
    zero, half, one = (jnp.asarray(c, BF16) for c in (0.0, 0.5, 1.0))
    gelu_c, gelu_a = jnp.asarray(0.7978845608028654, BF16), jnp.asarray(0.044715, BF16)
    ipb = et // nk
    assert ipb in (4, 8)
    group = pl.ds(pl.multiple_of((first_i // 8) * 8, 8), 8)
    upper_half = (first_i % 8) != 0

    def block_row(ref, h, r, ls):
        rows = ref[h, group, ls]
        if ipb == 8:
            return rows[r:r + 1]
        return jnp.where(upper_half, rows[ipb + r:ipb + r + 1], rows[r:r + 1])

    @pl.when(s == 0)
    def _():
        acc_sc[...] = jnp.zeros(acc_sc.shape, F32)
        y1_sc[...] = jnp.zeros((et, tm), F32)
        z0_sc[...] = jnp.zeros((et, tm), BF16)

    def step(y_new, y_old, z_new, z_old):
        d = vt_ref.shape[1]

        def scores(p):
            cs = slice(p * (tm // 2), (p + 1) * (tm // 2))
            y_new[:, cs] = _dot(u_ref[...], xnt_ref[:, cs])

        def gated(r, lc):
            rs = slice(r * nk, (r + 1) * nk)
            ls = slice(lc * LANES, (lc + 1) * LANES)
            w = jnp.zeros((nk, LANES), BF16)
            for h in range(PEER_HEADS):
                nb = block_row(n_ref, h, r, ls).astype(BF16)
                cb = block_row(coef_ref, h, r, ls).astype(BF16)
                gate = jnp.minimum(jnp.maximum(nb - rank2_ref[h, :, ls], zero), one)
                w = w + gate * (cb * e2_ref[h, :, ls])
            yy = y_old[rs, ls].astype(BF16)
            act = half * yy * (one + jnp.tanh(gelu_c * (yy + gelu_a * (yy * yy * yy))))
            z_new[rs, ls] = act * w

        def project(p):
            pr, pc = divmod(p, 2)
            rows = slice(pr * 256, (pr + 1) * 256)
            cs = slice(pc * 256, (pc + 1) * 256)
            acc_sc[rows, cs] += _dot(vt_ref[0, rows, :], z_old[:, cs])

        tiles = [(r, lc) for r in range(et // nk) for lc in range(tm // LANES)]
        for k, (r, lc) in enumerate(tiles):
            if k % 8 == 0:
                scores(k // 8)
            project(k)
            gated(r, lc)

    @pl.when(s % 2 == 0)
    def _():
        step(y0_sc, y1_sc, z1_sc, z0_sc)

    @pl.when(s % 2 == 1)
    def _():
        step(y1_sc, y0_sc, z0_sc, z1_sc)

    @pl.when(s == pl.num_programs(1) - 1)
    def _():
        o_ref[...] = h_ref[...] + acc_sc[...].T


def peer_dense(h, xnt, u, v, rank2, e2, n_i, coef, tm=512, et=512):
    t, d = h.shape
    nblk = u.shape[0] // et
    u = u.astype(BF16)
    vt = v.reshape(nblk, et, d).transpose(0, 2, 1).astype(BF16)
    last = nblk - 1
    blk = lambda s, lag: jnp.clip(s - lag, 0, last)
    tok = pl.BlockSpec((PEER_HEADS, PEER_NKEYS, tm), lambda i, s: (0, 0, i))
    return pl.pallas_call(
        functools.partial(_peer_dense_kernel, et=et, tm=tm),
        grid=(t // tm, nblk + 2),
        in_specs=[pl.BlockSpec((d, tm), lambda i, s: (0, i)),
                  pl.BlockSpec((et, d), lambda i, s: (blk(s, 0), 0)),
                  pl.BlockSpec((1, d, et), lambda i, s: (blk(s, 2), 0, 0)),
                  tok, tok, tok, tok,
                  pl.BlockSpec((tm, d), lambda i, s: (i, 0))],
        out_specs=pl.BlockSpec((tm, d), lambda i, s: (i, 0)),
        out_shape=jax.ShapeDtypeStruct((t, d), F32),
        scratch_shapes=[pltpu.VMEM((d, tm), F32), pltpu.VMEM((et, tm), F32), pltpu.VMEM((et, tm), F32),
                        pltpu.VMEM((et, tm), BF16), pltpu.VMEM((et, tm), BF16)],
        compiler_params=_cparams("parallel", "arbitrary"),
        name="peer_dense",
    )(xnt, u, vt, rank2, e2, n_i, coef, h)


def peer_layer(h, norm_g, w_q, keys, u, v):
    q, xnt = matmul(h, w_q.astype(BF16), gain=norm_g, emit_xnt=True)
    rank2, e2, n_i, coef = peer_select(q, keys)
    return peer_dense(h, xnt, u, v, rank2, e2, n_i, coef)


def _rmsnorm_kernel(x_ref, g_ref, o_ref):
    x = x_ref[...]
    o_ref[...] = x * lax.rsqrt(jnp.mean(x * x, axis=-1, keepdims=True) + EPS) * g_ref[...]


def rmsnorm_final(x2d, g, tm=512):
    t, d = x2d.shape
    return pl.pallas_call(
        _rmsnorm_kernel,
        grid=(t // tm,),
        in_specs=[pl.BlockSpec((tm, d), lambda i: (i, 0)),
                  pl.BlockSpec((1, d), lambda i: (0, 0))],
        out_specs=pl.BlockSpec((tm, d), lambda i: (i, 0)),
        out_shape=jax.ShapeDtypeStruct((t, d), F32),
        compiler_params=_cparams("parallel"),
        name="rmsnorm_final",
    )(x2d, g.reshape(1, d).astype(F32))


def _cd_w_in(w):
    widths = (C_DINNER, C_CONV_CH, C_HEADS, D_HEADS * D_DH, D_LATENT, IDX_HEADS * IDX_DIM, IDX_DIM, IDX_HEADS)
    offs = np.concatenate([[0], np.cumsum(widths)])
    z, xbc, dt, q, ckv, qidx, kidx, widx = [w[:, offs[i]:offs[i + 1]] for i in range(8)]
    k = w.shape[0]
    pad = lambda n: jnp.zeros((k, n), w.dtype)
    out = jnp.concatenate([z, xbc, q, qidx, ckv, kidx, pad(LANES - IDX_DIM),
                           dt, widx, pad(LANES - C_HEADS - IDX_HEADS)], axis=1)
    assert out.shape[1] == CD_WIDTH
    return out.astype(BF16)


def kernel(x, rel_table, ab_w_in, ab_w_out, ab_lambda, ab_a_norm, ab_b_norm, cd_w_in, cd_w_out, cd_conv_w, cd_conv_b, cd_dt_bias, cd_a_log, cd_d_skip, cd_ssm_norm, cd_kv_norm, cd_w_uk, cd_w_uv, norm_mix, norm_ffn, peer_w_q, peer_keys, peer_u, peer_v, norm_final):
    bsz, s_len, d = x.shape
    h = x.reshape(bsz * s_len, d)
    for layer in range(DEPTH):
        i = layer // 2
        if layer % 2 == 0:
            proj = matmul(h, ab_w_in[i].astype(BF16), gain=norm_mix[layer])
            oa = diff_attention(proj, rel_table, ab_lambda[i], ab_a_norm[i], bsz, s_len, layer)
            ob = retention(proj, ab_b_norm[i], bsz, s_len)
            mixed = jnp.concatenate([oa, ob], axis=-1)
            h = matmul(mixed, ab_w_out[i].astype(BF16), residual=h)
        else:
            proj = matmul(h, _cd_w_in(cd_w_in[i]), gain=norm_mix[layer])
            xconv = conv_silu(proj, cd_conv_w[i], cd_conv_b[i], bsz, s_len)
            yc = ssd_mixer(proj, xconv, cd_dt_bias[i], cd_a_log[i], cd_d_skip[i], cd_ssm_norm[i],
                           bsz, s_len)
            yd = dsa_mixer(proj, cd_kv_norm[i], cd_w_uk[i], cd_w_uv[i], rel_table, bsz, s_len)
            mixed = jnp.concatenate([yc, yd], axis=-1)
            h = matmul(mixed, cd_w_out[i].astype(BF16), residual=h)
        h = peer_layer(h, norm_ffn[layer], peer_w_q[layer], peer_keys[layer], peer_u[layer],
                       peer_v[layer])
    return rmsnorm_final(h, norm_final).reshape(bsz, s_len, d)
```

```python
import functools
import math

import jax
import jax.numpy as jnp
import numpy as np
from jax import lax
from jax.experimental import pallas as pl
from jax.experimental.pallas import tpu as pltpu

D_MODEL = 2048
DEPTH = 2
EPS = 1e-6
Q_BLOCK = 128
REL_BUCKETS = 32
REL_MAX_DIST = 128
A_HEADS = 8
A_DH = 64
A_DV = 2 * A_DH
B_HEADS = 8
B_DK = 64
B_DV = 128
RET_CHUNK = 128
ROPE_BASE = 10000.0
C_DINNER = D_MODEL
C_HEADDIM = 64
C_HEADS = C_DINNER // C_HEADDIM
C_GROUPS = 4
C_HPG = C_HEADS // C_GROUPS
C_DSTATE = 128
C_CONV = 4
C_CONV_CH = C_DINNER + 2 * C_GROUPS * C_DSTATE
SSD_CHUNK = 128
D_HEADS = 8
D_DH = 128
D_LATENT = 256
IDX_HEADS = 16
IDX_DIM = 64
IDX_TOPK_MAX = 256
PEER_HEADS = 8
PEER_NKEYS = 128
PEER_DQ = 256
PEER_TOPK = 16

F32 = jnp.float32
BF16 = jnp.bfloat16
NEG = -1e30
LANES = 128
VMEM_LIMIT = 56 * 1024 * 1024
HIGHEST = lax.Precision.HIGHEST

CD_Z = 0
CD_XBC = CD_Z + C_DINNER
CD_Q = CD_XBC + C_CONV_CH
CD_QIDX = CD_Q + D_HEADS * D_DH
CD_CKV = CD_QIDX + IDX_HEADS * IDX_DIM
CD_KIDX = CD_CKV + D_LATENT
CD_MISC = CD_KIDX + LANES
CD_WIDTH = CD_MISC + LANES
MISC_WIDX = C_HEADS


def _dot(a, b):
    return jnp.dot(a, b, preferred_element_type=F32)


def _dot_nt(a, b):
    return lax.dot_general(a, b, (((1,), (1,)), ((), ())), preferred_element_type=F32)


def _cparams(*sem):
    return pltpu.CompilerParams(dimension_semantics=sem, vmem_limit_bytes=VMEM_LIMIT)


def rel_bucket(dist):
    n = jnp.maximum(dist, 0)
    max_exact = REL_BUCKETS // 2
    nf = jnp.maximum(n, 1).astype(F32)
    large = max_exact + (jnp.log(nf / max_exact) / math.log(REL_MAX_DIST / max_exact)
                         * (REL_BUCKETS - max_exact)).astype(jnp.int32)
    large = jnp.minimum(large, REL_BUCKETS - 1)
    return jnp.where(n < max_exact, n, large)


def _near_bias(rel_table, tq):
    assert tq >= REL_MAX_DIST
    nh = rel_table.shape[1]
    dist = jnp.arange(-(tq - 1), 2 * tq)
    by_dist = rel_table[rel_bucket(dist)].astype(F32).T
    span = 3 * tq

    def first_row(k):
        ahead = by_dist[:, k - tq + 1:k + 1][:, ::-1]
        behind = by_dist[:, k + 1:k + tq][:, ::-1]
        return jnp.concatenate([ahead, jnp.zeros((nh, span - 2 * tq + 1), F32), behind], axis=1)

    rows = jnp.stack([first_row(tq - 1), first_row(2 * tq - 1)], axis=1)[:, :, None, :]
    tables = pl.pallas_call(
        functools.partial(_toeplitz_kernel, tq=tq),
        grid=(nh,),
        in_specs=[pl.BlockSpec((1, 2, 1, span), lambda h: (h, 0, 0, 0))],
        out_specs=pl.BlockSpec((1, 2, tq, tq), lambda h: (h, 0, 0, 0)),
        out_shape=jax.ShapeDtypeStruct((nh, 2, tq, tq), F32),
        compiler_params=_cparams("parallel"),
        name="bias_tables",
    )(rows)
    r = jnp.arange(tq)[:, None]
    c = jnp.arange(tq)[None, :]
    return tables, (r - c) >= 0


def _toeplitz_kernel(row_ref, o_ref, *, tq):
    for j in range(2):
        x = jnp.broadcast_to(row_ref[0, j], (tq, row_ref.shape[-1]))
        o_ref[0, j] = pltpu.roll(x, 0, 1, stride=1, stride_axis=0)[:, :tq]


def _mm_kernel(*refs, norm, residual, emit_xnt):
    it = iter(refs)
    x_ref = next(it)
    g_ref = next(it) if norm else None
    w_ref = next(it)
    r_ref = next(it) if residual else None
    o_ref = next(it)
    xo_ref = next(it) if emit_xnt else None
    xn_ref = next(it)

    @pl.when(pl.program_id(1) == 0)
    def _():
        x = x_ref[...].astype(F32)
        if norm:
            x = x * lax.rsqrt(jnp.mean(x * x, axis=-1, keepdims=True) + EPS) * g_ref[...]
        xn_ref[...] = x.astype(BF16)
        if emit_xnt:
            xo_ref[...] = x.T.astype(BF16)

    acc = _dot(xn_ref[...], w_ref[...])
    if residual:
        acc = acc + r_ref[...]
    o_ref[...] = acc.astype(o_ref.dtype)


def matmul(x, w, *, gain=None, residual=None, emit_xnt=False, out_dtype=F32, tm=1024, tn=512):
    t, k = x.shape
    n = w.shape[1]
    tm = min(tm, t)
    assert t % tm == 0 and n % tn == 0 and w.shape[0] == k
    norm = gain is not None
    res = residual is not None
    in_specs = [pl.BlockSpec((tm, k), lambda i, j: (i, 0))]
    args = [x]
    if norm:
        in_specs.append(pl.BlockSpec((1, k), lambda i, j: (0, 0)))
        args.append(gain.reshape(1, k).astype(F32))
    in_specs.append(pl.BlockSpec((k, tn), lambda i, j: (0, j)))
    args.append(w)
    if res:
        in_specs.append(pl.BlockSpec((tm, tn), lambda i, j: (i, j)))
        args.append(residual)
    out_specs = [pl.BlockSpec((tm, tn), lambda i, j: (i, j))]
    out_shape = [jax.ShapeDtypeStruct((t, n), out_dtype)]
    if emit_xnt:
        out_specs.append(pl.BlockSpec((k, tm), lambda i, j: (0, i)))
        out_shape.append(jax.ShapeDtypeStruct((k, t), BF16))
    outs = pl.pallas_call(
        functools.partial(_mm_kernel, norm=norm, residual=res, emit_xnt=emit_xnt),
        grid=(t // tm, n // tn),
        in_specs=in_specs,
        out_specs=out_specs,
        out_shape=out_shape,
        scratch_shapes=[pltpu.VMEM((tm, k), BF16)],
        compiler_params=_cparams("parallel", "arbitrary"),
        name="matmul",
    )(*args)
    return outs if emit_xnt else outs[0]


def _diffattn_kernel(far_ref, q_ref, k_ref, v_ref, bias_ref, lam_ref, g_ref, o_ref,
                     m_sc, l_sc, acc_sc, *, tq, lam_init):
    h = pl.program_id(1)
    qi = pl.program_id(2)
    q = (q_ref[...] * (A_DH ** -0.5)).astype(BF16)
    qs = (q[:, :A_DH], q[:, A_DH:])
    m_sc[...] = jnp.full(m_sc.shape, NEG, F32)
    l_sc[...] = jnp.zeros(l_sc.shape, F32)
    acc_sc[...] = jnp.zeros(acc_sc.shape, F32)

    def process(j, bias):
        rows = pl.ds(pl.multiple_of(j * tq, tq), tq)
        kb = k_ref[rows, :].astype(BF16)
        vb = v_ref[rows, :].astype(BF16)
        for m in range(2):
            s = _dot_nt(qs[m], kb[:, m * A_DH:(m + 1) * A_DH]) + bias
            m_prev = m_sc[m]
            m_new = jnp.maximum(m_prev, jnp.max(s, axis=-1, keepdims=True))
            alpha = jnp.exp(m_prev - m_new)
            p = jnp.exp(s - m_new)
            l_sc[m] = alpha * l_sc[m] + jnp.sum(p, axis=-1, keepdims=True)
            acc_sc[m] = alpha * acc_sc[m] + _dot(p.astype(BF16), vb)
            m_sc[m] = m_new

    far = far_ref[h]

    def far_body(j, c):
        process(j, far)
        return c

    lax.fori_loop(0, jnp.maximum(qi - 1, 0), far_body, 0)

    @pl.when(qi > 0)
    def _():
        process(qi - 1, bias_ref[0, 1])

    process(qi, bias_ref[0, 0])

    lp = lam_ref[...]
    lam = (jnp.exp(jnp.sum(lp[0:1] * lp[1:2], keepdims=True))
           - jnp.exp(jnp.sum(lp[2:3] * lp[3:4], keepdims=True)) + lam_init)
    o = acc_sc[0] / l_sc[0] - lam * (acc_sc[1] / l_sc[1])
    o = o * lax.rsqrt(jnp.mean(o * o, axis=-1, keepdims=True) + EPS) * g_ref[...] * (1.0 - lam_init)
    o_ref[...] = o.astype(o_ref.dtype)


def diff_attention(proj, rel_table, lam_p, a_norm, bsz, s_len, layer, tq=512):
    t = bsz * s_len
    nq = s_len // tq
    lam_init = 0.8 - 0.6 * math.exp(-0.3 * layer)
    bias, causal = _near_bias(rel_table, tq)
    bias = bias.at[:, 0].set(jnp.where(causal[None], bias[:, 0], NEG))
    far = rel_table[REL_BUCKETS - 1].astype(F32)
    kcol = A_HEADS * 2 * A_DH // LANES
    return pl.pallas_call(
        functools.partial(_diffattn_kernel, tq=tq, lam_init=lam_init),
        grid=(bsz, A_HEADS, nq),
        in_specs=[
            pl.BlockSpec(memory_space=pltpu.SMEM),
            pl.BlockSpec((tq, LANES), lambda b, h, i: (b * nq + i, h)),
            pl.BlockSpec((s_len, LANES), lambda b, h, i: (b, kcol + h)),
            pl.BlockSpec((s_len, LANES), lambda b, h, i: (b, 2 * kcol + h)),
            pl.BlockSpec((1, 2, tq, tq), lambda b, h, i: (h, 0, 0, 0)),
            pl.BlockSpec((4, A_DH), lambda b, h, i: (0, 0)),
            pl.BlockSpec((1, A_DV), lambda b, h, i: (0, 0)),
        ],
        out_specs=pl.BlockSpec((tq, LANES), lambda b, h, i: (b * nq + i, h)),
        out_shape=jax.ShapeDtypeStruct((t, A_HEADS * A_DV), BF16),
        scratch_shapes=[pltpu.VMEM((2, tq, 1), F32), pltpu.VMEM((2, tq, 1), F32),
                        pltpu.VMEM((2, tq, A_DV), F32)],
        compiler_params=_cparams("parallel", "parallel", "arbitrary"),
        name="diff_attention",
    )(far, proj, proj, proj, bias, lam_p.astype(F32), a_norm.reshape(1, A_DV).astype(F32))


def _retention_kernel(q_ref, k_ref, v_ref, gate_ref, cos_ref, sin_ref, inner_ref, qdec_ref, kdec_ref,
                      cdec_ref, g_ref, o_ref, *, nchunks):
    c = RET_CHUNK
    lane = lax.broadcasted_iota(jnp.int32, (c, LANES), 1)
    even = (lane % 2) == 0

    def rope(x, cos, sin):
        partner = jnp.where(even, pltpu.roll(x, LANES - 1, axis=1), pltpu.roll(x, 1, axis=1))
        return x * cos + partner * sin

    def body(ci, states):
        rows = pl.ds(pl.multiple_of(ci * c, c), c)
        cos = cos_ref[rows, :]
        sin = sin_ref[rows, :]
        qr = rope(q_ref[rows, :], cos, sin) * (B_DK ** -0.5)
        kr = rope(k_ref[rows, :], cos, sin)
        qd = (qr * qdec_ref[0]).astype(BF16)
        kdt = (kr * kdec_ref[0]).T.astype(BF16)
        qb = qr.astype(BF16)
        kb = kr.astype(BF16)
        new_states = []
        for hh in range(2):
            sl = slice(hh * B_DK, (hh + 1) * B_DK)
            vs = slice(hh * B_DV, (hh + 1) * B_DV)
            vv = v_ref[rows, vs].astype(BF16)
            sc = _dot_nt(qb[:, sl], kb[:, sl]) * inner_ref[hh]
            o = _dot(sc.astype(BF16), vv) + _dot(qd[:, sl], states[hh].astype(BF16))
            new_states.append(states[hh] * cdec_ref[hh][0:1, :] + _dot(kdt[sl, :], vv))
            o = o * lax.rsqrt(jnp.mean(o * o, axis=-1, keepdims=True) + EPS) * g_ref[...]
            gt = gate_ref[rows, vs]
            o_ref[rows, vs] = (o * (gt * jax.nn.sigmoid(gt))).astype(o_ref.dtype)
        return tuple(new_states)

    zero = jnp.zeros((B_DK, B_DV), F32)
    lax.fori_loop(0, nchunks, body, (zero, zero))


def retention(proj, b_norm, bsz, s_len):
    t = bsz * s_len
    c = RET_CHUNK
    nh = B_HEADS
    log_gamma = jnp.log(1.0 - 2.0 ** (-5.0 - jnp.arange(nh, dtype=F32)))
    idx = jnp.arange(c, dtype=F32)
    rel = idx[:, None] - idx[None, :]
    inner = jnp.where(rel[None] >= 0, jnp.exp(rel[None] * log_gamma[:, None, None]), 0.0)
    q_decay = jnp.exp((idx[:, None] + 1.0) * log_gamma[None, :])
    k_decay = jnp.exp((c - 1.0 - idx[:, None]) * log_gamma[None, :])
    chunk_decay = jnp.exp(c * log_gamma)

    def pair_lanes(d):
        return jnp.repeat(d.T.reshape(nh // 2, 2, c).transpose(0, 2, 1), B_DK, axis=-1)

    cdec = jnp.broadcast_to(chunk_decay[:, None, None], (nh, 8, B_DV))
    inv = ROPE_BASE ** (-jnp.arange(0, B_DK, 2, dtype=F32) / B_DK)
    ang = jnp.arange(s_len, dtype=F32)[:, None] * inv[None, :]
    sign = jnp.tile(jnp.asarray([-1.0, 1.0], F32), B_DK // 2)
    cos = jnp.tile(jnp.repeat(jnp.cos(ang), 2, axis=-1), (1, 2))
    sin = jnp.tile(jnp.repeat(jnp.sin(ang), 2, axis=-1) * sign, (1, 2))
    base = (A_HEADS * 2 * A_DH * 2 + A_HEADS * A_DV) // LANES
    kblk = base + B_HEADS * B_DK // LANES
    vblk = (kblk + B_HEADS * B_DK // LANES) // 2
    gblk = vblk + B_HEADS * B_DV // (2 * LANES)
    return pl.pallas_call(
        functools.partial(_retention_kernel, nchunks=s_len // c),
        grid=(bsz, nh // 2),
        in_specs=[
            pl.BlockSpec((s_len, LANES), lambda b, p: (b, base + p)),
            pl.BlockSpec((s_len, LANES), lambda b, p: (b, kblk + p)),
            pl.BlockSpec((s_len, 2 * B_DV), lambda b, p: (b, vblk + p)),
            pl.BlockSpec((s_len, 2 * B_DV), lambda b, p: (b, gblk + p)),
            pl.BlockSpec((s_len, LANES), lambda b, p: (0, 0)),
            pl.BlockSpec((s_len, LANES), lambda b, p: (0, 0)),
            pl.BlockSpec((2, c, c), lambda b, p: (p, 0, 0)),
            pl.BlockSpec((1, c, LANES), lambda b, p: (p, 0, 0)),
            pl.BlockSpec((1, c, LANES), lambda b, p: (p, 0, 0)),
            pl.BlockSpec((2, 8, B_DV), lambda b, p: (p, 0, 0)),
            pl.BlockSpec((1, B_DV), lambda b, p: (0, 0)),
        ],
        out_specs=pl.BlockSpec((s_len, 2 * B_DV), lambda b, p: (b, p)),
        out_shape=jax.ShapeDtypeStruct((t, nh * B_DV), BF16),
        compiler_params=_cparams("parallel", "parallel"),
        name="retention",
    )(proj, proj, proj, proj, cos, sin, inner, pair_lanes(q_decay), pair_lanes(k_decay), cdec,
      b_norm.reshape(1, B_DV).astype(F32))


def _conv_kernel(x_ref, w_ref, b_ref, o_ref):
    x = x_ref[...]
    row = lax.broadcasted_iota(jnp.int32, x.shape, 0)
    acc = x * w_ref[C_CONV - 1:C_CONV, :] + b_ref[...]
    for j in range(1, C_CONV):
        xs = jnp.where(row >= j, pltpu.roll(x, j, axis=0), 0.0)
        acc = acc + xs * w_ref[C_CONV - 1 - j:C_CONV - j, :]
    o_ref[...] = acc * jax.nn.sigmoid(acc)


def conv_silu(proj, conv_w, conv_b, bsz, s_len, tc=512):
    t = bsz * s_len
    off = CD_XBC // tc
    return pl.pallas_call(
        _conv_kernel,
        grid=(bsz, C_CONV_CH // tc),
        in_specs=[pl.BlockSpec((s_len, tc), lambda b, j: (b, off + j)),
                  pl.BlockSpec((C_CONV, tc), lambda b, j: (0, j)),
                  pl.BlockSpec((1, tc), lambda b, j: (0, j))],
        out_specs=pl.BlockSpec((s_len, tc), lambda b, j: (b, j)),
        out_shape=jax.ShapeDtypeStruct((t, C_CONV_CH), F32),
        compiler_params=_cparams("parallel", "parallel"),
        name="conv_silu",
    )(proj, conv_w.astype(F32), conv_b.reshape(1, C_CONV_CH).astype(F32))


def _ssd_kernel(x_ref, b_ref, c_ref, z_ref, dt_ref, dtb_ref, alog_ref, dsk_ref, ng_ref, o_ref,
                st_sc, y_sc, *, nchunks):
    qn = SSD_CHUNK
    p = C_HEADDIM
    r = lax.broadcasted_iota(jnp.int32, (qn, qn), 0)
    cc = lax.broadcasted_iota(jnp.int32, (qn, qn), 1)
    causal = cc <= r
    t1 = jnp.where(causal, 1.0, 0.0)
    t2 = jnp.where(r > cc, 1.0, 0.0)
    a = -jnp.exp(alog_ref[0])
    dsk = dsk_ref[0]
    st_sc[...] = jnp.zeros(st_sc.shape, F32)

    def body(ci, carry):
        rows = pl.ds(pl.multiple_of(ci * qn, qn), qn)
        xc = x_ref[rows, :]
        bc = b_ref[rows, :]
        cm = c_ref[rows, :]
        dtr = dt_ref[0, 0, rows, :] + dtb_ref[0]
        dt = jnp.maximum(dtr, 0.0) + jnp.log(1.0 + jnp.exp(-jnp.abs(dtr)))
        dta = dt * a
        cmb = cm.astype(BF16)
        cb = _dot_nt(cmb, bc.astype(BF16))
        bt = bc.T.astype(BF16)
        heads = range(C_HPG)
        cols = [dta[:, h:h + 1] for h in heads]
        segs = [jnp.dot(t1, cols[h] * t2, precision=HIGHEST, preferred_element_type=F32) for h in heads]
        css = [segs[h][:, 0:1] + cols[h][0:1, :] for h in heads]
        xhs = [xc[:, h * p:(h + 1) * p] for h in heads]
        xdts = [xhs[h] * dt[:, h:h + 1] for h in heads]
        sts = [st_sc[h] for h in heads]
        mats = [(cb * jnp.where(causal, jnp.exp(segs[h]), 0.0)).astype(BF16) for h in heads]
        intra = [_dot(mats[h], xdts[h].astype(BF16)) for h in heads]
        inter = [_dot(cmb, sts[h].astype(BF16)) for h in heads]
        lasts = [css[h][qn - 1:qn, :] for h in heads]
        upd = [_dot(bt, (xdts[h] * jnp.exp(lasts[h] - css[h])).astype(BF16)) for h in heads]
        for h in heads:
            st_sc[h] = sts[h] * jnp.exp(lasts[h]) + upd[h]
            y_sc[:, h * p:(h + 1) * p] = (intra[h] + inter[h] * jnp.exp(css[h])
                                          + xhs[h] * dsk[:, h:h + 1])
        zz = z_ref[rows, :]
        y = y_sc[...] * (zz * jax.nn.sigmoid(zz))
        y = y * lax.rsqrt(jnp.mean(y * y, axis=-1, keepdims=True) + EPS) * ng_ref[...]
        o_ref[rows, :] = y.astype(o_ref.dtype)
        return carry

    lax.fori_loop(0, nchunks, body, 0)


def ssd_mixer(proj, xconv, dt_bias, a_log, d_skip, norm_g, bsz, s_len):
    t = bsz * s_len
    g = C_GROUPS
    gw = C_DINNER // g
    dtg = proj[:, CD_MISC:CD_MISC + C_HEADS].reshape(bsz, s_len, g, C_HPG).transpose(0, 2, 1, 3)
    per_group = lambda v: v.astype(F32).reshape(g, 1, C_HPG)
    nb = C_DINNER // LANES
    return pl.pallas_call(
        functools.partial(_ssd_kernel, nchunks=s_len // SSD_CHUNK),
        grid=(bsz, g),
        in_specs=[
            pl.BlockSpec((s_len, gw), lambda b, k: (b, k)),
            pl.BlockSpec((s_len, C_DSTATE), lambda b, k: (b, nb + k)),
            pl.BlockSpec((s_len, C_DSTATE), lambda b, k: (b, nb + g + k)),
            pl.BlockSpec((s_len, gw), lambda b, k: (b, k)),
            pl.BlockSpec((1, 1, s_len, C_HPG), lambda b, k: (b, k, 0, 0)),
            pl.BlockSpec((1, 1, C_HPG), lambda b, k: (k, 0, 0)),
            pl.BlockSpec((1, 1, C_HPG), lambda b, k: (k, 0, 0)),
            pl.BlockSpec((1, 1, C_HPG), lambda b, k: (k, 0, 0)),
            pl.BlockSpec((1, gw), lambda b, k: (0, k)),
        ],
        out_specs=pl.BlockSpec((s_len, gw), lambda b, k: (b, k)),
        out_shape=jax.ShapeDtypeStruct((t, C_DINNER), BF16),
        scratch_shapes=[pltpu.VMEM((C_HPG, C_DSTATE, C_HEADDIM), F32),
                        pltpu.VMEM((SSD_CHUNK, gw), F32)],
        compiler_params=_cparams("parallel", "parallel"),
        name="ssd",
    )(xconv, xconv, xconv, proj, dtg, per_group(dt_bias), per_group(a_log), per_group(d_skip),
      norm_g.reshape(1, C_DINNER).astype(F32))


def _dsa_kernel(far_ref, q_ref, qidx_ref, misc_ref, ckv_ref, kidx_ref, wuk_ref, wuv_ref, kvn_ref,
                band_ref, o_ref, ckvn_sc, kidx_sc, key_sc, lg_sc, lg2_sc, mask_sc, *, widths, topk):
    tq = Q_BLOCK
    qi = pl.program_id(1)

    @pl.when(qi == 0)
    def _():
        c = ckv_ref[...]
        ckvn_sc[...] = (c * lax.rsqrt(jnp.mean(c * c, axis=-1, keepdims=True) + EPS)
                        * kvn_ref[...]).astype(BF16)
        kidx_sc[...] = kidx_ref[:, :IDX_DIM].astype(BF16)

    near = jnp.where(qi > 0, 1.0, 0.0)
    ur = lax.broadcasted_iota(jnp.int32, (LANES, LANES), 0)
    uc = lax.broadcasted_iota(jnp.int32, (LANES, LANES), 1)
    upper = jnp.where(ur <= uc, 1.0, 0.0).astype(BF16)
    kf = float(topk)
    int_min = jnp.int32(-2 ** 31)
    d0 = pl.ds(pl.multiple_of(qi * tq, tq), tq)
    d1 = pl.ds(pl.multiple_of(jnp.maximum(qi - 1, 0) * tq, tq), tq)

    def body(wd):
        qidx = qidx_ref[...].astype(BF16)
        w = misc_ref[:, MISC_WIDX:MISC_WIDX + IDX_HEADS] * ((IDX_HEADS * IDX_DIM) ** -0.5)
        kx = kidx_sc[:wd, :]
        sc = jnp.zeros((tq, wd), F32)
        for hi in range(IDX_HEADS):
            rel = _dot_nt(qidx[:, hi * IDX_DIM:(hi + 1) * IDX_DIM], kx)
            sc = sc + jnp.maximum(rel, 0.0) * w[:, hi:hi + 1]
        col = lax.broadcasted_iota(jnp.int32, (tq, wd), 1)
        row = lax.broadcasted_iota(jnp.int32, (tq, wd), 0) + qi * tq
        causal = col <= row
        sc = jnp.where(causal, sc, -jnp.inf)
        bits = pltpu.bitcast(sc, jnp.int32)
        key_sc[:, :wd] = jnp.where(bits < 0, bits ^ jnp.int32(0x7FFFFFFF), bits)

        def count_ge(cand):
            return jnp.sum(jnp.where(key_sc[:, :wd] >= cand, 1.0, 0.0), axis=-1, keepdims=True)

        zero = jnp.zeros((tq, 1), jnp.int32)
        prefix = jnp.where(count_ge(zero) >= kf, zero, zero + int_min)

        def bisect(i, prefix):
            cand = prefix | jnp.left_shift(jnp.int32(1), 30 - i)
            return jnp.where(count_ge(cand) >= kf, cand, prefix)

        thr = lax.fori_loop(0, 31, bisect, prefix)
        key = key_sc[:, :wd]
        gt = key > thr
        eq = key == thr
        need = kf - jnp.sum(jnp.where(gt, 1.0, 0.0), axis=-1, keepdims=True)
        eqf = jnp.where(eq, 1.0, 0.0)
        carry = jnp.zeros((tq, 1), F32)
        for j in range(wd // LANES):
            ls = slice(j * LANES, (j + 1) * LANES)
            e = eqf[:, ls]
            run = _dot(e.astype(BF16), upper) + carry
            carry = carry + jnp.sum(e, axis=-1, keepdims=True)
            take = jnp.where(gt[:, ls], 1.0, jnp.where(run <= need, e, 0.0))
            mask_sc[:, ls] = jnp.where(causal[:, ls], jnp.where(take > 0.5, 0.0, NEG), NEG)

        q = q_ref[...].astype(BF16)
        cw = 2 * LANES
        chunks = [slice(c0, min(c0 + cw, wd)) for c0 in range(0, wd, cw)]
        bufs = (lg_sc, lg2_sc)
        qa, top, den, ctx = {}, {}, {}, {}

        def logits_piece(h, ci):
            cs = chunks[ci]
            buf = bufs[h % 2]
            if ci == 0:
                qa[h] = _dot(q[:, h * D_DH:(h + 1) * D_DH], wuk_ref[h]).astype(BF16)
            lg = (_dot_nt(qa[h], ckvn_sc[cs, :]) * (D_DH ** -0.5)
                  + (mask_sc[:, cs] + far_ref[h]))
            buf[:, cs] = lg
            mx = jnp.max(lg, axis=-1, keepdims=True)
            top[h] = mx if ci == 0 else jnp.maximum(top[h], mx)
            if ci == len(chunks) - 1:
                buf[:, d0] += band_ref[h, 0]
                buf[:, d1] += band_ref[h, 1] * near
                top[h] = jnp.maximum(top[h], jnp.maximum(jnp.max(buf[:, d0], axis=-1, keepdims=True),
                                                         jnp.max(buf[:, d1], axis=-1, keepdims=True)))

        def context_piece(h, ci):
            cs = chunks[ci]
            pr = jnp.exp(bufs[h % 2][:, cs] - top[h])
            part = jnp.sum(pr, axis=-1, keepdims=True)
            den[h] = part if ci == 0 else den[h] + part
            upd = _dot(pr.astype(BF16), ckvn_sc[cs, :])
            ctx[h] = upd if ci == 0 else ctx[h] + upd
            if ci == len(chunks) - 1:
                o_ref[:, h * D_DH:(h + 1) * D_DH] = _dot((ctx[h] / den[h]).astype(BF16),
                                                         wuv_ref[h]).astype(o_ref.dtype)

        for h in range(D_HEADS + 1):
            for ci in range(len(chunks)):
                if h < D_HEADS:
                    logits_piece(h, ci)
                if h >= 1:
                    context_piece(h - 1, ci)

    hi_key = (qi + 1) * tq
    lo = 0
    for wd in widths:
        @pl.when((hi_key > lo) & (hi_key <= wd))
        def _():
            body(wd)
        lo = wd


def _key_widths(s_len, tq, levels=4):
    nq = s_len // tq
    return tuple(sorted({-(-nq * (k + 1) // levels) * tq for k in range(levels)}))


def dsa_mixer(proj, kv_norm, w_uk, w_uv, rel_table, bsz, s_len):
    t = bsz * s_len
    tq = Q_BLOCK
    nq = s_len // tq
    topk = min(IDX_TOPK_MAX, s_len // 4)
    bias, causal = _near_bias(rel_table, tq)
    far = rel_table[REL_BUCKETS - 1].astype(F32)
    band = bias - far[:, None, None, None]
    band = band.at[:, 0].set(jnp.where(causal[None], band[:, 0], 0.0))
    hw = D_HEADS * D_DH
    return pl.pallas_call(
        functools.partial(_dsa_kernel, widths=_key_widths(s_len, tq), topk=topk),
        grid=(bsz, nq),
        in_specs=[
            pl.BlockSpec(memory_space=pltpu.SMEM),
            pl.BlockSpec((tq, hw), lambda b, i: (b * nq + i, CD_Q // hw)),
            pl.BlockSpec((tq, hw), lambda b, i: (b * nq + i, CD_QIDX // hw)),
            pl.BlockSpec((tq, LANES), lambda b, i: (b * nq + i, CD_MISC // LANES)),
            pl.BlockSpec((s_len, D_LATENT), lambda b, i: (b, CD_CKV // D_LATENT)),
            pl.BlockSpec((s_len, LANES), lambda b, i: (b, CD_KIDX // LANES)),
            pl.BlockSpec((D_HEADS, D_DH, D_LATENT), lambda b, i: (0, 0, 0)),
            pl.BlockSpec((D_HEADS, D_LATENT, D_DH), lambda b, i: (0, 0, 0)),
            pl.BlockSpec((1, D_LATENT), lambda b, i: (0, 0)),
            pl.BlockSpec((D_HEADS, 2, tq, tq), lambda b, i: (0, 0, 0, 0)),
        ],
        out_specs=pl.BlockSpec((tq, hw), lambda b, i: (b * nq + i, 0)),
        out_shape=jax.ShapeDtypeStruct((t, hw), BF16),
        scratch_shapes=[pltpu.VMEM((s_len, D_LATENT), BF16), pltpu.VMEM((s_len, IDX_DIM), BF16),
                        pltpu.VMEM((tq, s_len), jnp.int32), pltpu.VMEM((tq, s_len), F32),
                        pltpu.VMEM((tq, s_len), F32), pltpu.VMEM((tq, s_len), F32)],
        compiler_params=_cparams("parallel", "arbitrary"),
        name="dsa",
    )(far, proj, proj, proj, proj, proj, w_uk.astype(BF16), w_uv.astype(BF16),
      kv_norm.reshape(1, D_LATENT).astype(F32), band)


def _peer_tables():
    pairs = [(k1, k2) for k1 in range(PEER_TOPK) for k2 in range(PEER_TOPK)
             if (k1 + 1) * (k2 + 1) <= PEER_TOPK]
    n = PEER_NKEYS
    r1 = np.zeros((n, n), np.float32)
    r2 = np.zeros((n, n), np.float32)
    pad = np.full((n, 1), NEG, np.float32)
    for r, (k1, k2) in enumerate(pairs):
        r1[r, k1] = 1.0
        r2[r, k2] = 1.0
        pad[r, 0] = 0.0
    return jnp.asarray(r1), jnp.asarray(r2), jnp.asarray(pad), jnp.asarray(r1.T, dtype=BF16)


def _peer_select_kernel(q_ref, keys_ref, r1_ref, r2_ref, pad_ref, grp_ref,
                        rank2_ref, e2_ref, n_ref, coef_ref):
    tm = q_ref.shape[0]
    nk = PEER_NKEYS
    ridx = lax.broadcasted_iota(jnp.int32, (nk, tm), 0).astype(F32)
    kidx = lax.broadcasted_iota(jnp.int32, (PEER_TOPK, tm), 0).astype(F32)

    def extract(chains, track, by_index):
        ss = list(chains)
        ranks = [jnp.full((nk, tm), 99.0, F32) for _ in ss]
        vals = [jnp.zeros((PEER_TOPK, tm), F32) for _ in ss]
        for k in range(PEER_TOPK):
            ms = [jnp.max(s, axis=0, keepdims=True) for s in ss]
            if by_index:
                firsts = [jnp.min(jnp.where(s == m, ridx, 1e9), axis=0, keepdims=True)
                          for s, m in zip(ss, ms)]
                hits = [ridx == f for f in firsts]
            else:
                hits = [s == m for s, m in zip(ss, ms)]
            ss = [jnp.where(hit, -jnp.inf, s) for s, hit in zip(ss, hits)]
            if track:
                ranks = [jnp.where(hit, float(k), r) for r, hit in zip(ranks, hits)]
                vals = [jnp.where(kidx == float(k), m, v) for v, m in zip(vals, ms)]
        return ss, ranks, vals

    half = PEER_DQ // 2
    nh = q_ref.shape[1] // PEER_DQ
    pad_rows = jnp.zeros((nk - PEER_TOPK, tm), F32)
    scores = []
    for hh in range(nh):
        q = q_ref[:, hh * PEER_DQ:(hh + 1) * PEER_DQ].astype(BF16)
        scores.append(_dot_nt(keys_ref[hh, 0], q[:, :half]))
        scores.append(_dot_nt(keys_ref[hh, 1], q[:, half:]))
    def run(by_index):
        _, ranks, tops = extract(scores, True, by_index)
        cand0s = []
        for hh in range(nh):
            a1p = jnp.concatenate([tops[2 * hh], pad_rows], axis=0)
            a2p = jnp.concatenate([tops[2 * hh + 1], pad_rows], axis=0)
            cand0s.append(jnp.dot(r1_ref[...], a1p, precision=HIGHEST, preferred_element_type=F32)
                          + jnp.dot(r2_ref[...], a2p, precision=HIGHEST, preferred_element_type=F32)
                          + pad_ref[...])
        cands, _, _ = extract(cand0s, False, by_index)
        ties = jnp.zeros((1, tm), F32)
        for hh in range(nh):
            s1, s2 = scores[2 * hh], scores[2 * hh + 1]
            rank1, rank2 = ranks[2 * hh], ranks[2 * hh + 1]
            a1, a2 = tops[2 * hh], tops[2 * hh + 1]
            cand0, cand = cand0s[hh], cands[hh]
            taken = jnp.where((cand == -jnp.inf) & (cand0 > 0.5 * NEG), 1.0, 0.0)
            cnt = _dot(grp_ref[...], taken.astype(BF16))
            top = a1[0:1, :] + a2[0:1, :]
            zsum = jnp.sum(taken * jnp.exp(cand0 - top), axis=0, keepdims=True)
            n_i = jnp.zeros((nk, tm), F32)
            for k in range(PEER_TOPK):
                n_i = jnp.where(rank1 == float(k), cnt[k:k + 1, :], n_i)
            rank2_ref[hh] = rank2.astype(rank2_ref.dtype)
            e2_ref[hh] = jnp.exp(s2 - a2[0:1, :]).astype(e2_ref.dtype)
            n_ref[hh] = n_i
            coef_ref[hh] = jnp.exp(s1 - a1[0:1, :]) / zsum
            if not by_index:
                for removed in (jnp.where(rank1 < 99.0, 1.0, 0.0), jnp.where(rank2 < 99.0, 1.0, 0.0),
                                taken):
                    n_removed = jnp.sum(removed, axis=0, keepdims=True)
                    ties = ties + jnp.where(n_removed != float(PEER_TOPK), 1.0, 0.0)
        return ties

    ties = run(False)

    @pl.when(jnp.max(ties) > 0.0)
    def _():
        run(True)


def peer_select(q, keys, tm=128, hpb=2):
    t = q.shape[0]
    r1, r2, pad, grp = _peer_tables()
    nk = PEER_NKEYS
    full = lambda shape: pl.BlockSpec(shape, lambda i, h: (0,) * len(shape))
    out_spec = pl.BlockSpec((hpb, nk, tm), lambda i, h: (h, 0, i))
    sds = lambda dt: jax.ShapeDtypeStruct((PEER_HEADS, nk, t), dt)
    return pl.pallas_call(
        _peer_select_kernel,
        grid=(t // tm, PEER_HEADS // hpb),
        in_specs=[pl.BlockSpec((tm, hpb * PEER_DQ), lambda i, h: (i, h)),
                  pl.BlockSpec((hpb, 2, nk, PEER_DQ // 2), lambda i, h: (h, 0, 0, 0)),
                  full((nk, nk)), full((nk, nk)), full((nk, 1)), full((nk, nk))],
        out_specs=[out_spec] * 4,
        out_shape=[sds(BF16), sds(BF16), sds(F32), sds(F32)],
        compiler_params=_cparams("parallel", "parallel"),
        name="peer_select",
    )(q, keys.astype(BF16), r1, r2, pad, grp)


def _peer_dense_kernel(xnt_ref, u_ref, vt_ref, rank2_ref, e2_ref, n_ref, coef_ref, h_ref, o_ref,
                       acc_sc, y0_sc, y1_sc, z0_sc, z1_sc, *, et, tm):
    s = pl.program_id(1)
    nk = PEER_NKEYS
    nblk = pl.num_programs(1) - 2
    first_i = jnp.clip(s - 1, 0, nblk - 1) * (et // nk)
    zero, half, one = (jnp.asarray(c, BF16) for c in (0.0, 0.5, 1.0))
    gelu_c, gelu_a = jnp.asarray(0.7978845608028654, BF16), jnp.asarray(0.044715, BF16)
    ipb = et // nk
    assert ipb in (4, 8)
    group = pl.ds(pl.multiple_of((first_i // 8) * 8, 8), 8)
    upper_half = (first_i % 8) != 0

    def block_row(ref, h, r, ls):
        rows = ref[h, group, ls]
        if ipb == 8:
            return rows[r:r + 1]
        return jnp.where(upper_half, rows[ipb + r:ipb + r + 1], rows[r:r + 1])

    @pl.when(s == 0)
    def _():
        acc_sc[...] = jnp.zeros(acc_sc.shape, F32)
        y1_sc[...] = jnp.zeros((et, tm), F32)
        z0_sc[...] = jnp.zeros((et, tm), BF16)

    def step(y_new, y_old, z_new, z_old):
        d = vt_ref.shape[1]

        def scores(p):
            cs = slice(p * (tm // 2), (p + 1) * (tm // 2))
            y_new[:, cs] = _dot(u_ref[...], xnt_ref[:, cs])

        def gated(r, lc):
            rs = slice(r * nk, (r + 1) * nk)
            ls = slice(lc * LANES, (lc + 1) * LANES)
            w = jnp.zeros((nk, LANES), BF16)
            for h in range(PEER_HEADS):
                nb = block_row(n_ref, h, r, ls).astype(BF16)
                cb = block_row(coef_ref, h, r, ls).astype(BF16)
                gate = jnp.minimum(jnp.maximum(nb - rank2_ref[h, :, ls], zero), one)
                w = w + gate * (cb * e2_ref[h, :, ls])
            yy = y_old[rs, ls].astype(BF16)
            act = half * yy * (one + jnp.tanh(gelu_c * (yy + gelu_a * (yy * yy * yy))))
            z_new[rs, ls] = act * w

        def project(p):
            pr, pc = divmod(p, 2)
            rows = slice(pr * 256, (pr + 1) * 256)
            cs = slice(pc * 256, (pc + 1) * 256)
            acc_sc[rows, cs] += _dot(vt_ref[0, rows, :], z_old[:, cs])

        tiles = [(r, lc) for r in range(et // nk) for lc in range(tm // LANES)]
        for k, (r, lc) in enumerate(tiles):
            if k % 8 == 0:
                scores(k // 8)
            project(k)
            gated(r, lc)

    @pl.when(s % 2 == 0)
    def _():
        step(y0_sc, y1_sc, z1_sc, z0_sc)

    @pl.when(s % 2 == 1)
    def _():
        step(y1_sc, y0_sc, z0_sc, z1_sc)

    @pl.when(s == pl.num_programs(1) - 1)
    def _():
        o_ref[...] = h_ref[...] + acc_sc[...].T


def peer_dense(h, xnt, u, v, rank2, e2, n_i, coef, tm=512, et=512):
    t, d = h.shape
    nblk = u.shape[0] // et
    u = u.astype(BF16)
    vt = v.reshape(nblk, et, d).transpose(0, 2, 1).astype(BF16)
    last = nblk - 1
    blk = lambda s, lag: jnp.clip(s - lag, 0, last)
    tok = pl.BlockSpec((PEER_HEADS, PEER_NKEYS, tm), lambda i, s: (0, 0, i))
    return pl.pallas_call(
        functools.partial(_peer_dense_kernel, et=et, tm=tm),
        grid=(t // tm, nblk + 2),
        in_specs=[pl.BlockSpec((d, tm), lambda i, s: (0, i)),
                  pl.BlockSpec((et, d), lambda i, s: (blk(s, 0), 0)),
                  pl.BlockSpec((1, d, et), lambda i, s: (blk(s, 2), 0, 0)),
                  tok, tok, tok, tok,
                  pl.BlockSpec((tm, d), lambda i, s: (i, 0))],
        out_specs=pl.BlockSpec((tm, d), lambda i, s: (i, 0)),
        out_shape=jax.ShapeDtypeStruct((t, d), F32),
        scratch_shapes=[pltpu.VMEM((d, tm), F32), pltpu.VMEM((et, tm), F32), pltpu.VMEM((et, tm), F32),
                        pltpu.VMEM((et, tm), BF16), pltpu.VMEM((et, tm), BF16)],
        compiler_params=_cparams("parallel", "arbitrary"),
        name="peer_dense",
    )(xnt, u, vt, rank2, e2, n_i, coef, h)


def peer_layer(h, norm_g, w_q, keys, u, v):
    q, xnt = matmul(h, w_q.astype(BF16), gain=norm_g, emit_xnt=True)
    rank2, e2, n_i, coef = peer_select(q, keys)
    return peer_dense(h, xnt, u, v, rank2, e2, n_i, coef)


def _rmsnorm_kernel(x_ref, g_ref, o_ref):
    x = x_ref[...]
    o_ref[...] = x * lax.rsqrt(jnp.mean(x * x, axis=-1, keepdims=True) + EPS) * g_ref[...]


def rmsnorm_final(x2d, g, tm=512):
    t, d = x2d.shape
    return pl.pallas_call(
        _rmsnorm_kernel,
        grid=(t // tm,),
        in_specs=[pl.BlockSpec((tm, d), lambda i: (i, 0)),
                  pl.BlockSpec((1, d), lambda i: (0, 0))],
        out_specs=pl.BlockSpec((tm, d), lambda i: (i, 0)),
        out_shape=jax.ShapeDtypeStruct((t, d), F32),
        compiler_params=_cparams("parallel"),
        name="rmsnorm_final",
    )(x2d, g.reshape(1, d).astype(F32))


def _cd_w_in(w):
    widths = (C_DINNER, C_CONV_CH, C_HEADS, D_HEADS * D_DH, D_LATENT, IDX_HEADS * IDX_DIM, IDX_DIM, IDX_HEADS)
    offs = np.concatenate([[0], np.cumsum(widths)])
    z, xbc, dt, q, ckv, qidx, kidx, widx = [w[:, offs[i]:offs[i + 1]] for i in range(8)]
    k = w.shape[0]
    pad = lambda n: jnp.zeros((k, n), w.dtype)
    out = jnp.concatenate([z, xbc, q, qidx, ckv, kidx, pad(LANES - IDX_DIM),
                           dt, widx, pad(LANES - C_HEADS - IDX_HEADS)], axis=1)
    assert out.shape[1] == CD_WIDTH
    return out.astype(BF16)


def kernel(x, rel_table, ab_w_in, ab_w_out, ab_lambda, ab_a_norm, ab_b_norm, cd_w_in, cd_w_out, cd_conv_w, cd_conv_b, cd_dt_bias, cd_a_log, cd_d_skip, cd_ssm_norm, cd_kv_norm, cd_w_uk, cd_w_uv, norm_mix, norm_ffn, peer_w_q, peer_keys, peer_u, peer_v, norm_final):
    bsz, s_len, d = x.shape
    h = x.reshape(bsz * s_len, d)
    for layer in range(DEPTH):
        i = layer // 2
        if layer % 2 == 0:
            proj = matmul(h, ab_w_in[i].astype(BF16), gain=norm_mix[layer])
            oa = diff_attention(proj, rel_table, ab_lambda[i], ab_a_norm[i], bsz, s_len, layer)
            ob = retention(proj, ab_b_norm[i], bsz, s_len)
            mixed = jnp.concatenate([oa, ob], axis=-1)
            h = matmul(mixed, ab_w_out[i].astype(BF16), residual=h)
        else:
            proj = matmul(h, _cd_w_in(cd_w_in[i]), gain=norm_mix[layer])
            xconv = conv_silu(proj, cd_conv_w[i], cd_conv_b[i], bsz, s_len)
            yc = ssd_mixer(proj, xconv, cd_dt_bias[i], cd_a_log[i], cd_d_skip[i], cd_ssm_norm[i],
                           bsz, s_len)
            yd = dsa_mixer(proj, cd_kv_norm[i], cd_w_uk[i], cd_w_uv[i], rel_table, bsz, s_len)
            mixed = jnp.concatenate([yc, yd], axis=-1)
            h = matmul(mixed, cd_w_out[i].astype(BF16), residual=h)
        h = peer_layer(h, norm_ffn[layer], peer_w_q[layer], peer_keys[layer], peer_u[layer],
                       peer_v[layer])
    return rmsnorm_final(h, norm_final).reshape(bsz, s_len, d)
```

```python
import functools
import math

import jax
import jax.numpy as jnp
import numpy as np
from jax import lax
from jax.experimental import pallas as pl
from jax.experimental.pallas import tpu as pltpu

D_MODEL = 2048
DEPTH = 2
EPS = 1e-6
Q_BLOCK = 128
REL_BUCKETS = 32
REL_MAX_DIST = 128
A_HEADS = 8
A_DH = 64
A_DV = 2 * A_DH
B_HEADS = 8
B_DK = 64
B_DV = 128
RET_CHUNK = 128
ROPE_BASE = 10000.0
C_DINNER = D_MODEL
C_HEADDIM = 64
C_HEADS = C_DINNER // C_HEADDIM
C_GROUPS = 4
C_HPG = C_HEADS // C_GROUPS
C_DSTATE = 128
C_CONV = 4
C_CONV_CH = C_DINNER + 2 * C_GROUPS * C_DSTATE
SSD_CHUNK = 128
D_HEADS = 8
D_DH = 128
D_LATENT = 256
IDX_HEADS = 16
IDX_DIM = 64
IDX_TOPK_MAX = 256
PEER_HEADS = 8
PEER_NKEYS = 128
PEER_DQ = 256
PEER_TOPK = 16

F32 = jnp.float32
BF16 = jnp.bfloat16
NEG = -1e30
LANES = 128
VMEM_LIMIT = 56 * 1024 * 1024
HIGHEST = lax.Precision.HIGHEST

CD_Z = 0
CD_XBC = CD_Z + C_DINNER
CD_Q = CD_XBC + C_CONV_CH
CD_QIDX = CD_Q + D_HEADS * D_DH
CD_CKV = CD_QIDX + IDX_HEADS * IDX_DIM
CD_KIDX = CD_CKV + D_LATENT
CD_MISC = CD_KIDX + LANES
CD_WIDTH = CD_MISC + LANES
MISC_WIDX = C_HEADS


def _dot(a, b):
    return jnp.dot(a, b, preferred_element_type=F32)


def _dot_nt(a, b):
    return lax.dot_general(a, b, (((1,), (1,)), ((), ())), preferred_element_type=F32)


def _cparams(*sem):
    return pltpu.CompilerParams(dimension_semantics=sem, vmem_limit_bytes=VMEM_LIMIT)


def rel_bucket(dist):
    n = jnp.maximum(dist, 0)
    max_exact = REL_BUCKETS // 2
    nf = jnp.maximum(n, 1).astype(F32)
    large = max_exact + (jnp.log(nf / max_exact) / math.log(REL_MAX_DIST / max_exact)
                         * (REL_BUCKETS - max_exact)).astype(jnp.int32)
    large = jnp.minimum(large, REL_BUCKETS - 1)
    return jnp.where(n < max_exact, n, large)


def _near_bias(rel_table, tq):
    assert tq >= REL_MAX_DIST
    nh = rel_table.shape[1]
    dist = jnp.arange(-(tq - 1), 2 * tq)
    by_dist = rel_table[rel_bucket(dist)].astype(F32).T
    span = 3 * tq

    def first_row(k):
        ahead = by_dist[:, k - tq + 1:k + 1][:, ::-1]
        behind = by_dist[:, k + 1:k + tq][:, ::-1]
        return jnp.concatenate([ahead, jnp.zeros((nh, span - 2 * tq + 1), F32), behind], axis=1)

    rows = jnp.stack([first_row(tq - 1), first_row(2 * tq - 1)], axis=1)[:, :, None, :]
    tables = pl.pallas_call(
        functools.partial(_toeplitz_kernel, tq=tq),
        grid=(nh,),
        in_specs=[pl.BlockSpec((1, 2, 1, span), lambda h: (h, 0, 0, 0))],
        out_specs=pl.BlockSpec((1, 2, tq, tq), lambda h: (h, 0, 0, 0)),
        out_shape=jax.ShapeDtypeStruct((nh, 2, tq, tq), F32),
        compiler_params=_cparams("parallel"),
        name="bias_tables",
    )(rows)
    r = jnp.arange(tq)[:, None]
    c = jnp.arange(tq)[None, :]
    return tables, (r - c) >= 0


def _toeplitz_kernel(row_ref, o_ref, *, tq):
    for j in range(2):
        x = jnp.broadcast_to(row_ref[0, j], (tq, row_ref.shape[-1]))
        o_ref[0, j] = pltpu.roll(x, 0, 1, stride=1, stride_axis=0)[:, :tq]


def _mm_kernel(*refs, norm, residual, emit_xnt):
    it = iter(refs)
    x_ref = next(it)
    g_ref = next(it) if norm else None
    w_ref = next(it)
    r_ref = next(it) if residual else None
    o_ref = next(it)
    xo_ref = next(it) if emit_xnt else None
    xn_ref = next(it)

    @pl.when(pl.program_id(1) == 0)
    def _():
        x = x_ref[...].astype(F32)
        if norm:
            x = x * lax.rsqrt(jnp.mean(x * x, axis=-1, keepdims=True) + EPS) * g_ref[...]
        xn_ref[...] = x.astype(BF16)
        if emit_xnt:
            xo_ref[...] = x.T.astype(BF16)

    acc = _dot(xn_ref[...], w_ref[...])
    if residual:
        acc = acc + r_ref[...]
    o_ref[...] = acc.astype(o_ref.dtype)


def matmul(x, w, *, gain=None, residual=None, emit_xnt=False, out_dtype=F32, tm=1024, tn=512):
    t, k = x.shape
    n = w.shape[1]
    tm = min(tm, t)
    assert t % tm == 0 and n % tn == 0 and w.shape[0] == k
    norm = gain is not None
    res = residual is not None
    in_specs = [pl.BlockSpec((tm, k), lambda i, j: (i, 0))]
    args = [x]
    if norm:
        in_specs.append(pl.BlockSpec((1, k), lambda i, j: (0, 0)))
        args.append(gain.reshape(1, k).astype(F32))
    in_specs.append(pl.BlockSpec((k, tn), lambda i, j: (0, j)))
    args.append(w)
    if res:
        in_specs.append(pl.BlockSpec((tm, tn), lambda i, j: (i, j)))
        args.append(residual)
    out_specs = [pl.BlockSpec((tm, tn), lambda i, j: (i, j))]
    out_shape = [jax.ShapeDtypeStruct((t, n), out_dtype)]
    if emit_xnt:
        out_specs.append(pl.BlockSpec((k, tm), lambda i, j: (0, i)))
        out_shape.append(jax.ShapeDtypeStruct((k, t), BF16))
    outs = pl.pallas_call(
        functools.partial(_mm_kernel, norm=norm, residual=res, emit_xnt=emit_xnt),
        grid=(t // tm, n // tn),
        in_specs=in_specs,
        out_specs=out_specs,
        out_shape=out_shape,
        scratch_shapes=[pltpu.VMEM((tm, k), BF16)],
        compiler_params=_cparams("parallel", "arbitrary"),
        name="matmul",
    )(*args)
    return outs if emit_xnt else outs[0]


def _diffattn_kernel(far_ref, q_ref, k_ref, v_ref, bias_ref, lam_ref, g_ref, o_ref,
                     m_sc, l_sc, acc_sc, *, tq, lam_init):
    h = pl.program_id(1)
    qi = pl.program_id(2)
    q = (q_ref[...] * (A_DH ** -0.5)).astype(BF16)
    qs = (q[:, :A_DH], q[:, A_DH:])
    m_sc[...] = jnp.full(m_sc.shape, NEG, F32)
    l_sc[...] = jnp.zeros(l_sc.shape, F32)
    acc_sc[...] = jnp.zeros(acc_sc.shape, F32)

    def process(j, bias):
        rows = pl.ds(pl.multiple_of(j * tq, tq), tq)
        kb = k_ref[rows, :].astype(BF16)
        vb = v_ref[rows, :].astype(BF16)
        for m in range(2):
            s = _dot_nt(qs[m], kb[:, m * A_DH:(m + 1) * A_DH]) + bias
            m_prev = m_sc[m]
            m_new = jnp.maximum(m_prev, jnp.max(s, axis=-1, keepdims=True))
            alpha = jnp.exp(m_prev - m_new)
            p = jnp.exp(s - m_new)
            l_sc[m] = alpha * l_sc[m] + jnp.sum(p, axis=-1, keepdims=True)
            acc_sc[m] = alpha * acc_sc[m] + _dot(p.astype(BF16), vb)
            m_sc[m] = m_new

    far = far_ref[h]

    def far_body(j, c):
        process(j, far)
        return c

    lax.fori_loop(0, jnp.maximum(qi - 1, 0), far_body, 0)

    @pl.when(qi > 0)
    def _():
        process(qi - 1, bias_ref[0, 1])

    process(qi, bias_ref[0, 0])

    lp = lam_ref[...]
    lam = (jnp.exp(jnp.sum(lp[0:1] * lp[1:2], keepdims=True))
           - jnp.exp(jnp.sum(lp[2:3] * lp[3:4], keepdims=True)) + lam_init)
    o = acc_sc[0] / l_sc[0] - lam * (acc_sc[1] / l_sc[1])
    o = o * lax.rsqrt(jnp.mean(o * o, axis=-1, keepdims=True) + EPS) * g_ref[...] * (1.0 - lam_init)
    o_ref[...] = o.astype(o_ref.dtype)


def diff_attention(proj, rel_table, lam_p, a_norm, bsz, s_len, layer, tq=512):
    t = bsz * s_len
    nq = s_len // tq
    lam_init = 0.8 - 0.6 * math.exp(-0.3 * layer)
    bias, causal = _near_bias(rel_table, tq)
    bias = bias.at[:, 0].set(jnp.where(causal[None], bias[:, 0], NEG))
    far = rel_table[REL_BUCKETS - 1].astype(F32)
    kcol = A_HEADS * 2 * A_DH // LANES
    return pl.pallas_call(
        functools.partial(_diffattn_kernel, tq=tq, lam_init=lam_init),
        grid=(bsz, A_HEADS, nq),
        in_specs=[
            pl.BlockSpec(memory_space=pltpu.SMEM),
            pl.BlockSpec((tq, LANES), lambda b, h, i: (b * nq + i, h)),
            pl.BlockSpec((s_len, LANES), lambda b, h, i: (b, kcol + h)),
            pl.BlockSpec((s_len, LANES), lambda b, h, i: (b, 2 * kcol + h)),
            pl.BlockSpec((1, 2, tq, tq), lambda b, h, i: (h, 0, 0, 0)),
            pl.BlockSpec((4, A_DH), lambda b, h, i: (0, 0)),
            pl.BlockSpec((1, A_DV), lambda b, h, i: (0, 0)),
        ],
        out_specs=pl.BlockSpec((tq, LANES), lambda b, h, i: (b * nq + i, h)),
        out_shape=jax.ShapeDtypeStruct((t, A_HEADS * A_DV), BF16),
        scratch_shapes=[pltpu.VMEM((2, tq, 1), F32), pltpu.VMEM((2, tq, 1), F32),
                        pltpu.VMEM((2, tq, A_DV), F32)],
        compiler_params=_cparams("parallel", "parallel", "arbitrary"),
        name="diff_attention",
    )(far, proj, proj, proj, bias, lam_p.astype(F32), a_norm.reshape(1, A_DV).astype(F32))


def _retention_kernel(q_ref, k_ref, v_ref, gate_ref, cos_ref, sin_ref, inner_ref, qdec_ref, kdec_ref,
                      cdec_ref, g_ref, o_ref, *, nchunks):
    c = RET_CHUNK
    lane = lax.broadcasted_iota(jnp.int32, (c, LANES), 1)
    even = (lane % 2) == 0

    def rope(x, cos, sin):
        partner = jnp.where(even, pltpu.roll(x, LANES - 1, axis=1), pltpu.roll(x, 1, axis=1))
        return x * cos + partner * sin

    def body(ci, states):
        rows = pl.ds(pl.multiple_of(ci * c, c), c)
        cos = cos_ref[rows, :]
        sin = sin_ref[rows, :]
        qr = rope(q_ref[rows, :], cos, sin) * (B_DK ** -0.5)
        kr = rope(k_ref[rows, :], cos, sin)
        qd = (qr * qdec_ref[0]).astype(BF16)
        kdt = (kr * kdec_ref[0]).T.astype(BF16)
        qb = qr.astype(BF16)
        kb = kr.astype(BF16)
        new_states = []
        for hh in range(2):
            sl = slice(hh * B_DK, (hh + 1) * B_DK)
            vs = slice(hh * B_DV, (hh + 1) * B_DV)
            vv = v_ref[rows, vs].astype(BF16)
            sc = _dot_nt(qb[:, sl], kb[:, sl]) * inner_ref[hh]
            o = _dot(sc.astype(BF16), vv) + _dot(qd[:, sl], states[hh].astype(BF16))
            new_states.append(states[hh] * cdec_ref[hh][0:1, :] + _dot(kdt[sl, :], vv))
            o = o * lax.rsqrt(jnp.mean(o * o, axis=-1, keepdims=True) + EPS) * g_ref[...]
            gt = gate_ref[rows, vs]
            o_ref[rows, vs] = (o * (gt * jax.nn.sigmoid(gt))).astype(o_ref.dtype)
        return tuple(new_states)

    zero = jnp.zeros((B_DK, B_DV), F32)
    lax.fori_loop(0, nchunks, body, (zero, zero))


def retention(proj, b_norm, bsz, s_len):
    t = bsz * s_len
    c = RET_CHUNK
    nh = B_HEADS
    log_gamma = jnp.log(1.0 - 2.0 ** (-5.0 - jnp.arange(nh, dtype=F32)))
    idx = jnp.arange(c, dtype=F32)
    rel = idx[:, None] - idx[None, :]
    inner = jnp.where(rel[None] >= 0, jnp.exp(rel[None] * log_gamma[:, None, None]), 0.0)
    q_decay = jnp.exp((idx[:, None] + 1.0) * log_gamma[None, :])
    k_decay = jnp.exp((c - 1.0 - idx[:, None]) * log_gamma[None, :])
    chunk_decay = jnp.exp(c * log_gamma)

    def pair_lanes(d):
        return jnp.repeat(d.T.reshape(nh // 2, 2, c).transpose(0, 2, 1), B_DK, axis=-1)

    cdec = jnp.broadcast_to(chunk_decay[:, None, None], (nh, 8, B_DV))
    inv = ROPE_BASE ** (-jnp.arange(0, B_DK, 2, dtype=F32) / B_DK)
    ang = jnp.arange(s_len, dtype=F32)[:, None] * inv[None, :]
    sign = jnp.tile(jnp.asarray([-1.0, 1.0], F32), B_DK // 2)
    cos = jnp.tile(jnp.repeat(jnp.cos(ang), 2, axis=-1), (1, 2))
    sin = jnp.tile(jnp.repeat(jnp.sin(ang), 2, axis=-1) * sign, (1, 2))
    base = (A_HEADS * 2 * A_DH * 2 + A_HEADS * A_DV) // LANES
    kblk = base + B_HEADS * B_DK // LANES
    vblk = (kblk + B_HEADS * B_DK // LANES) // 2
    gblk = vblk + B_HEADS * B_DV // (2 * LANES)
    return pl.pallas_call(
        functools.partial(_retention_kernel, nchunks=s_len // c),
        grid=(bsz, nh // 2),
        in_specs=[
            pl.BlockSpec((s_len, LANES), lambda b, p: (b, base + p)),
            pl.BlockSpec((s_len, LANES), lambda b, p: (b, kblk + p)),
            pl.BlockSpec((s_len, 2 * B_DV), lambda b, p: (b, vblk + p)),
            pl.BlockSpec((s_len, 2 * B_DV), lambda b, p: (b, gblk + p)),
            pl.BlockSpec((s_len, LANES), lambda b, p: (0, 0)),
            pl.BlockSpec((s_len, LANES), lambda b, p: (0, 0)),
            pl.BlockSpec((2, c, c), lambda b, p: (p, 0, 0)),
            pl.BlockSpec((1, c, LANES), lambda b, p: (p, 0, 0)),
            pl.BlockSpec((1, c, LANES), lambda b, p: (p, 0, 0)),
            pl.BlockSpec((2, 8, B_DV), lambda b, p: (p, 0, 0)),
            pl.BlockSpec((1, B_DV), lambda b, p: (0, 0)),
        ],
        out_specs=pl.BlockSpec((s_len, 2 * B_DV), lambda b, p: (b, p)),
        out_shape=jax.ShapeDtypeStruct((t, nh * B_DV), BF16),
        compiler_params=_cparams("parallel", "parallel"),
        name="retention",
    )(proj, proj, proj, proj, cos, sin, inner, pair_lanes(q_decay), pair_lanes(k_decay), cdec,
      b_norm.reshape(1, B_DV).astype(F32))


def _conv_kernel(x_ref, w_ref, b_ref, o_ref):
    x = x_ref[...]
    row = lax.broadcasted_iota(jnp.int32, x.shape, 0)
    acc = x * w_ref[C_CONV - 1:C_CONV, :] + b_ref[...]
    for j in range(1, C_CONV):
        xs = jnp.where(row >= j, pltpu.roll(x, j, axis=0), 0.0)
        acc = acc + xs * w_ref[C_CONV - 1 - j:C_CONV - j, :]
    o_ref[...] = acc * jax.nn.sigmoid(acc)


def conv_silu(proj, conv_w, conv_b, bsz, s_len, tc=512):
    t = bsz * s_len
    off = CD_XBC // tc
    return pl.pallas_call(
        _conv_kernel,
        grid=(bsz, C_CONV_CH // tc),
        in_specs=[pl.BlockSpec((s_len, tc), lambda b, j: (b, off + j)),
                  pl.BlockSpec((C_CONV, tc), lambda b, j: (0, j)),
                  pl.BlockSpec((1, tc), lambda b, j: (0, j))],
        out_specs=pl.BlockSpec((s_len, tc), lambda b, j: (b, j)),
        out_shape=jax.ShapeDtypeStruct((t, C_CONV_CH), F32),
        compiler_params=_cparams("parallel", "parallel"),
        name="conv_silu",
    )(proj, conv_w.astype(F32), conv_b.reshape(1, C_CONV_CH).astype(F32))


def _ssd_kernel(x_ref, b_ref, c_ref, z_ref, dt_ref, dtb_ref, alog_ref, dsk_ref, ng_ref, o_ref,
                st_sc, y_sc, *, nchunks):
    qn = SSD_CHUNK
    p = C_HEADDIM
    r = lax.broadcasted_iota(jnp.int32, (qn, qn), 0)
    cc = lax.broadcasted_iota(jnp.int32, (qn, qn), 1)
    causal = cc <= r
    t1 = jnp.where(causal, 1.0, 0.0)
    t2 = jnp.where(r > cc, 1.0, 0.0)
    a = -jnp.exp(alog_ref[0])
    dsk = dsk_ref[0]
    st_sc[...] = jnp.zeros(st_sc.shape, F32)

    def body(ci, carry):
        rows = pl.ds(pl.multiple_of(ci * qn, qn), qn)
        xc = x_ref[rows, :]
        bc = b_ref[rows, :]
        cm = c_ref[rows, :]
        dtr = dt_ref[0, 0, rows, :] + dtb_ref[0]
        dt = jnp.maximum(dtr, 0.0) + jnp.log(1.0 + jnp.exp(-jnp.abs(dtr)))
        dta = dt * a
        cmb = cm.astype(BF16)
        cb = _dot_nt(cmb, bc.astype(BF16))
        bt = bc.T.astype(BF16)
        heads = range(C_HPG)
        cols = [dta[:, h:h + 1] for h in heads]
        segs = [jnp.dot(t1, cols[h] * t2, precision=HIGHEST, preferred_element_type=F32) for h in heads]
        css = [segs[h][:, 0:1] + cols[h][0:1, :] for h in heads]
        xhs = [xc[:, h * p:(h + 1) * p] for h in heads]
        xdts = [xhs[h] * dt[:, h:h + 1] for h in heads]
        sts = [st_sc[h] for h in heads]
        mats = [(cb * jnp.where(causal, jnp.exp(segs[h]), 0.0)).astype(BF16) for h in heads]
        intra = [_dot(mats[h], xdts[h].astype(BF16)) for h in heads]
        inter = [_dot(cmb, sts[h].astype(BF16)) for h in heads]
        lasts = [css[h][qn - 1:qn, :] for h in heads]
        upd = [_dot(bt, (xdts[h] * jnp.exp(lasts[h] - css[h])).astype(BF16)) for h in heads]
        for h in heads:
            st_sc[h] = sts[h] * jnp.exp(lasts[h]) + upd[h]
            y_sc[:, h * p:(h + 1) * p] = (intra[h] + inter[h] * jnp.exp(css[h])
                                          + xhs[h] * dsk[:, h:h + 1])
        zz = z_ref[rows, :]
        y = y_sc[...] * (zz * jax.nn.sigmoid(zz))
        y = y * lax.rsqrt(jnp.mean(y * y, axis=-1, keepdims=True) + EPS) * ng_ref[...]
        o_ref[rows, :] = y.astype(o_ref.dtype)
        return carry

    lax.fori_loop(0, nchunks, body, 0)


def ssd_mixer(proj, xconv, dt_bias, a_log, d_skip, norm_g, bsz, s_len):
    t = bsz * s_len
    g = C_GROUPS
    gw = C_DINNER // g
    dtg = proj[:, CD_MISC:CD_MISC + C_HEADS].reshape(bsz, s_len, g, C_HPG).transpose(0, 2, 1, 3)
    per_group = lambda v: v.astype(F32).reshape(g, 1, C_HPG)
    nb = C_DINNER // LANES
    return pl.pallas_call(
        functools.partial(_ssd_kernel, nchunks=s_len // SSD_CHUNK),
        grid=(bsz, g),
        in_specs=[
            pl.BlockSpec((s_len, gw), lambda b, k: (b, k)),
            pl.BlockSpec((s_len, C_DSTATE), lambda b, k: (b, nb + k)),
            pl.BlockSpec((s_len, C_DSTATE), lambda b, k: (b, nb + g + k)),
            pl.BlockSpec((s_len, gw), lambda b, k: (b, k)),
            pl.BlockSpec((1, 1, s_len, C_HPG), lambda b, k: (b, k, 0, 0)),
            pl.BlockSpec((1, 1, C_HPG), lambda b, k: (k, 0, 0)),
            pl.BlockSpec((1, 1, C_HPG), lambda b, k: (k, 0, 0)),
            pl.BlockSpec((1, 1, C_HPG), lambda b, k: (k, 0, 0)),
            pl.BlockSpec((1, gw), lambda b, k: (0, k)),
        ],
        out_specs=pl.BlockSpec((s_len, gw), lambda b, k: (b, k)),
        out_shape=jax.ShapeDtypeStruct((t, C_DINNER), BF16),
        scratch_shapes=[pltpu.VMEM((C_HPG, C_DSTATE, C_HEADDIM), F32),
                        pltpu.VMEM((SSD_CHUNK, gw), F32)],
        compiler_params=_cparams("parallel", "parallel"),
        name="ssd",
    )(xconv, xconv, xconv, proj, dtg, per_group(dt_bias), per_group(a_log), per_group(d_skip),
      norm_g.reshape(1, C_DINNER).astype(F32))


def _dsa_kernel(far_ref, q_ref, qidx_ref, misc_ref, ckv_ref, kidx_ref, wuk_ref, wuv_ref, kvn_ref,
                band_ref, o_ref, ckvn_sc, kidx_sc, key_sc, lg_sc, lg2_sc, mask_sc, *, widths, topk):
    tq = Q_BLOCK
    qi = pl.program_id(1)

    @pl.when(qi == 0)
    def _():
        c = ckv_ref[...]
        ckvn_sc[...] = (c * lax.rsqrt(jnp.mean(c * c, axis=-1, keepdims=True) + EPS)
                        * kvn_ref[...]).astype(BF16)
        kidx_sc[...] = kidx_ref[:, :IDX_DIM].astype(BF16)

    near = jnp.where(qi > 0, 1.0, 0.0)
    ur = lax.broadcasted_iota(jnp.int32, (LANES, LANES), 0)
    uc = lax.broadcasted_iota(jnp.int32, (LANES, LANES), 1)
    upper = jnp.where(ur <= uc, 1.0, 0.0).astype(BF16)
    kf = float(topk)
    int_min = jnp.int32(-2 ** 31)
    d0 = pl.ds(pl.multiple_of(qi * tq, tq), tq)
    d1 = pl.ds(pl.multiple_of(jnp.maximum(qi - 1, 0) * tq, tq), tq)

    def body(wd):
        qidx = qidx_ref[...].astype(BF16)
        w = misc_ref[:, MISC_WIDX:MISC_WIDX + IDX_HEADS] * ((IDX_HEADS * IDX_DIM) ** -0.5)
        kx = kidx_sc[:wd, :]
        sc = jnp.zeros((tq, wd), F32)
        for hi in range(IDX_HEADS):
            rel = _dot_nt(qidx[:, hi * IDX_DIM:(hi + 1) * IDX_DIM], kx)
            sc = sc + jnp.maximum(rel, 0.0) * w[:, hi:hi + 1]
        col = lax.broadcasted_iota(jnp.int32, (tq, wd), 1)
        row = lax.broadcasted_iota(jnp.int32, (tq, wd), 0) + qi * tq
        causal = col <= row
        sc = jnp.where(causal, sc, -jnp.inf)
        bits = pltpu.bitcast(sc, jnp.int32)
        key_sc[:, :wd] = jnp.where(bits < 0, bits ^ jnp.int32(0x7FFFFFFF), bits)

        def count_ge(cand):
            return jnp.sum(jnp.where(key_sc[:, :wd] >= cand, 1.0, 0.0), axis=-1, keepdims=True)

        zero = jnp.zeros((tq, 1), jnp.int32)
        prefix = jnp.where(count_ge(zero) >= kf, zero, zero + int_min)

        def bisect(i, prefix):
            cand = prefix | jnp.left_shift(jnp.int32(1), 30 - i)
            return jnp.where(count_ge(cand) >= kf, cand, prefix)

        thr = lax.fori_loop(0, 31, bisect, prefix)
        key = key_sc[:, :wd]
        gt = key > thr
        eq = key == thr
        need = kf - jnp.sum(jnp.where(gt, 1.0, 0.0), axis=-1, keepdims=True)
        eqf = jnp.where(eq, 1.0, 0.0)
        carry = jnp.zeros((tq, 1), F32)
        for j in range(wd // LANES):
            ls = slice(j * LANES, (j + 1) * LANES)
            e = eqf[:, ls]
            run = _dot(e.astype(BF16), upper) + carry
            carry = carry + jnp.sum(e, axis=-1, keepdims=True)
            take = jnp.where(gt[:, ls], 1.0, jnp.where(run <= need, e, 0.0))
            mask_sc[:, ls] = jnp.where(causal[:, ls], jnp.where(take > 0.5, 0.0, NEG), NEG)

        q = q_ref[...].astype(BF16)
        cw = 2 * LANES
        chunks = [slice(c0, min(c0 + cw, wd)) for c0 in range(0, wd, cw)]
        bufs = (lg_sc, lg2_sc)
        qa, top, den, ctx = {}, {}, {}, {}

        def logits_piece(h, ci):
            cs = chunks[ci]
            buf = bufs[h % 2]
            if ci == 0:
                qa[h] = _dot(q[:, h * D_DH:(h + 1) * D_DH], wuk_ref[h]).astype(BF16)
            lg = (_dot_nt(qa[h], ckvn_sc[cs, :]) * (D_DH ** -0.5)
                  + (mask_sc[:, cs] + far_ref[h]))
            buf[:, cs] = lg
            mx = jnp.max(lg, axis=-1, keepdims=True)
            top[h] = mx if ci == 0 else jnp.maximum(top[h], mx)
            if ci == len(chunks) - 1:
                buf[:, d0] += band_ref[h, 0]
                buf[:, d1] += band_ref[h, 1] * near
                top[h] = jnp.maximum(top[h], jnp.maximum(jnp.max(buf[:, d0], axis=-1, keepdims=True),
                                                         jnp.max(buf[:, d1], axis=-1, keepdims=True)))

        def context_piece(h, ci):
            cs = chunks[ci]
            pr = jnp.exp(bufs[h % 2][:, cs] - top[h])
            part = jnp.sum(pr, axis=-1, keepdims=True)
            den[h] = part if ci == 0 else den[h] + part
            upd = _dot(pr.astype(BF16), ckvn_sc[cs, :])
            ctx[h] = upd if ci == 0 else ctx[h] + upd
            if ci == len(chunks) - 1:
                o_ref[:, h * D_DH:(h + 1) * D_DH] = _dot((ctx[h] / den[h]).astype(BF16),
                                                         wuv_ref[h]).astype(o_ref.dtype)

        for h in range(D_HEADS + 1):
            for ci in range(len(chunks)):
                if h < D_HEADS:
                    logits_piece(h, ci)
                if h >= 1:
                    context_piece(h - 1, ci)

    hi_key = (qi + 1) * tq
    lo = 0
    for wd in widths:
        @pl.when((hi_key > lo) & (hi_key <= wd))
        def _():
            body(wd)
        lo = wd


def _key_widths(s_len, tq, levels=4):
    nq = s_len // tq
    return tuple(sorted({-(-nq * (k + 1) // levels) * tq for k in range(levels)}))


def dsa_mixer(proj, kv_norm, w_uk, w_uv, rel_table, bsz, s_len):
    t = bsz * s_len
    tq = Q_BLOCK
    nq = s_len // tq
    topk = min(IDX_TOPK_MAX, s_len // 4)
    bias, causal = _near_bias(rel_table, tq)
    far = rel_table[REL_BUCKETS - 1].astype(F32)
    band = bias - far[:, None, None, None]
    band = band.at[:, 0].set(jnp.where(causal[None], band[:, 0], 0.0))
    hw = D_HEADS * D_DH
    return pl.pallas_call(
        functools.partial(_dsa_kernel, widths=_key_widths(s_len, tq), topk=topk),
        grid=(bsz, nq),
        in_specs=[
            pl.BlockSpec(memory_space=pltpu.SMEM),
            pl.BlockSpec((tq, hw), lambda b, i: (b * nq + i, CD_Q // hw)),
            pl.BlockSpec((tq, hw), lambda b, i: (b * nq + i, CD_QIDX // hw)),
            pl.BlockSpec((tq, LANES), lambda b, i: (b * nq + i, CD_MISC // LANES)),
            pl.BlockSpec((s_len, D_LATENT), lambda b, i: (b, CD_CKV // D_LATENT)),
            pl.BlockSpec((s_len, LANES), lambda b, i: (b, CD_KIDX // LANES)),
            pl.BlockSpec((D_HEADS, D_DH, D_LATENT), lambda b, i: (0, 0, 0)),
            pl.BlockSpec((D_HEADS, D_LATENT, D_DH), lambda b, i: (0, 0, 0)),
            pl.BlockSpec((1, D_LATENT), lambda b, i: (0, 0)),
            pl.BlockSpec((D_HEADS, 2, tq, tq), lambda b, i: (0, 0, 0, 0)),
        ],
        out_specs=pl.BlockSpec((tq, hw), lambda b, i: (b * nq + i, 0)),
        out_shape=jax.ShapeDtypeStruct((t, hw), BF16),
        scratch_shapes=[pltpu.VMEM((s_len, D_LATENT), BF16), pltpu.VMEM((s_len, IDX_DIM), BF16),
                        pltpu.VMEM((tq, s_len), jnp.int32), pltpu.VMEM((tq, s_len), F32),
                        pltpu.VMEM((tq, s_len), F32), pltpu.VMEM((tq, s_len), F32)],
        compiler_params=_cparams("parallel", "arbitrary"),
        name="dsa",
    )(far, proj, proj, proj, proj, proj, w_uk.astype(BF16), w_uv.astype(BF16),
      kv_norm.reshape(1, D_LATENT).astype(F32), band)


def _peer_tables():
    pairs = [(k1, k2) for k1 in range(PEER_TOPK) for k2 in range(PEER_TOPK)
             if (k1 + 1) * (k2 + 1) <= PEER_TOPK]
    n = PEER_NKEYS
    r1 = np.zeros((n, n), np.float32)
    r2 = np.zeros((n, n), np.float32)
    pad = np.full((n, 1), NEG, np.float32)
    for r, (k1, k2) in enumerate(pairs):
        r1[r, k1] = 1.0
        r2[r, k2] = 1.0
        pad[r, 0] = 0.0
    return jnp.asarray(r1), jnp.asarray(r2), jnp.asarray(pad), jnp.asarray(r1.T, dtype=BF16)


def _peer_select_kernel(q_ref, keys_ref, r1_ref, r2_ref, pad_ref, grp_ref,
                        rank2_ref, e2_ref, n_ref, coef_ref):
    tm = q_ref.shape[0]
    nk = PEER_NKEYS
    ridx = lax.broadcasted_iota(jnp.int32, (nk, tm), 0).astype(F32)
    kidx = lax.broadcasted_iota(jnp.int32, (PEER_TOPK, tm), 0).astype(F32)

    def extract(chains, track, by_index):
        ss = list(chains)
        ranks = [jnp.full((nk, tm), 99.0, F32) for _ in ss]
        vals = [jnp.zeros((PEER_TOPK, tm), F32) for _ in ss]
        for k in range(PEER_TOPK):
            ms = [jnp.max(s, axis=0, keepdims=True) for s in ss]
            if by_index:
                firsts = [jnp.min(jnp.where(s == m, ridx, 1e9), axis=0, keepdims=True)
                          for s, m in zip(ss, ms)]
                hits = [ridx == f for f in firsts]
            else:
                hits = [s == m for s, m in zip(ss, ms)]
            ss = [jnp.where(hit, -jnp.inf, s) for s, hit in zip(ss, hits)]
            if track:
                ranks = [jnp.where(hit, float(k), r) for r, hit in zip(ranks, hits)]
                vals = [jnp.where(kidx == float(k), m, v) for v, m in zip(vals, ms)]
        return ss, ranks, vals

    half = PEER_DQ // 2
    nh = q_ref.shape[1] // PEER_DQ
    pad_rows = jnp.zeros((nk - PEER_TOPK, tm), F32)
    scores = []
    for hh in range(nh):
        q = q_ref[:, hh * PEER_DQ:(hh + 1) * PEER_DQ].astype(BF16)
        scores.append(_dot_nt(keys_ref[hh, 0], q[:, :half]))
        scores.append(_dot_nt(keys_ref[hh, 1], q[:, half:]))
    def run(by_index):
        _, ranks, tops = extract(scores, True, by_index)
        cand0s = []
        for hh in range(nh):
            a1p = jnp.concatenate([tops[2 * hh], pad_rows], axis=0)
            a2p = jnp.concatenate([tops[2 * hh + 1], pad_rows], axis=0)
            cand0s.append(jnp.dot(r1_ref[...], a1p, precision=HIGHEST, preferred_element_type=F32)
                          + jnp.dot(r2_ref[...], a2p, precision=HIGHEST, preferred_element_type=F32)
                          + pad_ref[...])
        cands, _, _ = extract(cand0s, False, by_index)
        ties = jnp.zeros((1, tm), F32)
        for hh in range(nh):
            s1, s2 = scores[2 * hh], scores[2 * hh + 1]
            rank1, rank2 = ranks[2 * hh], ranks[2 * hh + 1]
            a1, a2 = tops[2 * hh], tops[2 * hh + 1]
            cand0, cand = cand0s[hh], cands[hh]
            taken = jnp.where((cand == -jnp.inf) & (cand0 > 0.5 * NEG), 1.0, 0.0)
            cnt = _dot(grp_ref[...], taken.astype(BF16))
            top = a1[0:1, :] + a2[0:1, :]
            zsum = jnp.sum(taken * jnp.exp(cand0 - top), axis=0, keepdims=True)
            n_i = jnp.zeros((nk, tm), F32)
            for k in range(PEER_TOPK):
                n_i = jnp.where(rank1 == float(k), cnt[k:k + 1, :], n_i)
            rank2_ref[hh] = rank2.astype(rank2_ref.dtype)
            e2_ref[hh] = jnp.exp(s2 - a2[0:1, :]).astype(e2_ref.dtype)
            n_ref[hh] = n_i.reshape(n_ref.shape[1:])
            coef_ref[hh] = (jnp.exp(s1 - a1[0:1, :]) / zsum).reshape(coef_ref.shape[1:])
            if not by_index:
                for removed in (jnp.where(rank1 < 99.0, 1.0, 0.0), jnp.where(rank2 < 99.0, 1.0, 0.0),
                                taken):
                    n_removed = jnp.sum(removed, axis=0, keepdims=True)
                    ties = ties + jnp.where(n_removed != float(PEER_TOPK), 1.0, 0.0)
        return ties

    ties = run(False)

    @pl.when(jnp.max(ties) > 0.0)
    def _():
        run(True)


def peer_select(q, keys, tm=256, hpb=PEER_HEADS):
    t = q.shape[0]
    r1, r2, pad, grp = _peer_tables()
    nk = PEER_NKEYS
    full = lambda shape: pl.BlockSpec(shape, lambda i, h: (0,) * len(shape))
    out_spec = pl.BlockSpec((hpb, nk, tm), lambda i, h: (h, 0, i))
    grp_spec = pl.BlockSpec((hpb, nk // 8, 8, tm), lambda i, h: (h, 0, 0, i))
    sds = lambda dt: jax.ShapeDtypeStruct((PEER_HEADS, nk, t), dt)
    grp_sds = jax.ShapeDtypeStruct((PEER_HEADS, nk // 8, 8, t), F32)
    return pl.pallas_call(
        _peer_select_kernel,
        grid=(t // tm, PEER_HEADS // hpb),
        in_specs=[pl.BlockSpec((tm, hpb * PEER_DQ), lambda i, h: (i, h)),
                  pl.BlockSpec((hpb, 2, nk, PEER_DQ // 2), lambda i, h: (h, 0, 0, 0)),
                  full((nk, nk)), full((nk, nk)), full((nk, 1)), full((nk, nk))],
        out_specs=[out_spec, out_spec, grp_spec, grp_spec],
        out_shape=[sds(BF16), sds(BF16), grp_sds, grp_sds],
        compiler_params=_cparams("parallel", "parallel"),
        name="peer_select",
    )(q, keys.astype(BF16), r1, r2, pad, grp)


def _peer_dense_kernel(xnt_ref, u_ref, vt_ref, rank2_ref, e2_ref, n_ref, coef_ref, h_ref, o_ref,
                       acc_sc, y0_sc, y1_sc, z0_sc, z1_sc, *, et, tm):
    s = pl.program_id(1)
    nk = PEER_NKEYS
    nblk = pl.num_programs(1) - 2
    ipb = et // nk
    assert ipb in (4, 8)
    first_i = jnp.clip(s - 1, 0, nblk - 1) * ipb
    upper_half = (first_i % 8) != 0
    zero, half, one = (jnp.asarray(c, BF16) for c in (0.0, 0.5, 1.0))
    gelu_c, gelu_a = jnp.asarray(0.7978845608028654, BF16), jnp.asarray(0.044715, BF16)

    def block_rows(ref, h, ls):
        rows = ref[h, 0, :, ls]
        if ipb < 8:
            rows = jnp.where(upper_half, rows[ipb:], rows[:ipb])
        return rows.astype(BF16)

    @pl.when(s == 0)
    def _():
        acc_sc[...] = jnp.zeros(acc_sc.shape, F32)
        y1_sc[...] = jnp.zeros((et, tm), F32)
        z0_sc[...] = jnp.zeros((et, tm), BF16)

    def step(y_new, y_old, z_new, z_old):
        d = vt_ref.shape[1]
        piece = 2 * LANES

        def scores(p):
            cs = slice(p * piece, (p + 1) * piece)
            y_new[:, cs] = _dot(u_ref[...], xnt_ref[:, cs])

        per_token = {}

        def rows_of(ref, h, lc, ls):
            if (id(ref), h, lc) not in per_token:
                per_token[(id(ref), h, lc)] = block_rows(ref, h, ls)
            return per_token[(id(ref), h, lc)]

        def gated(r, lc):
            rs = slice(r * nk, (r + 1) * nk)
            ls = slice(lc * LANES, (lc + 1) * LANES)
            w = jnp.zeros((nk, LANES), BF16)
            for h in range(PEER_HEADS):
                nb = rows_of(n_ref, h, lc, ls)[r:r + 1]
                cb = rows_of(coef_ref, h, lc, ls)[r:r + 1]
                gate = jnp.minimum(jnp.maximum(nb - rank2_ref[h, :, ls], zero), one)
                w = w + gate * (cb * e2_ref[h, :, ls])
            yy = y_old[rs, ls].astype(BF16)
            act = half * yy * (one + jnp.tanh(gelu_c * (yy + gelu_a * (yy * yy * yy))))
            z_new[rs, ls] = act * w

        def project(p):
            pr, pc = divmod(p, tm // piece)
            rows = slice(pr * piece, (pr + 1) * piece)
            cs = slice(pc * piece, (pc + 1) * piece)
            acc_sc[rows, cs] += _dot(vt_ref[0, rows, :], z_old[:, cs])

        tiles = [(r, lc) for lc in range(tm // LANES) for r in range(ipb)]
        per_proj = len(tiles) // ((d // piece) * (tm // piece))
        per_score = len(tiles) // (tm // piece)
        assert per_proj >= 1 and per_score >= 1
        for k, (r, lc) in enumerate(tiles):
            if k % per_score == 0:
                scores(k // per_score)
            if k % per_proj == 0:
                project(k // per_proj)
            gated(r, lc)

    @pl.when(s % 2 == 0)
    def _():
        step(y0_sc, y1_sc, z1_sc, z0_sc)

    @pl.when(s % 2 == 1)
    def _():
        step(y1_sc, y0_sc, z0_sc, z1_sc)

    @pl.when(s == pl.num_programs(1) - 1)
    def _():
        o_ref[...] = h_ref[...] + acc_sc[...].T


def peer_dense(h, xnt, u, v, rank2, e2, n_g, coef_g, tm=512, et=512):
    t, d = h.shape
    tm = min(tm, t)
    nblk = u.shape[0] // et
    ipb = et // PEER_NKEYS
    u = u.astype(BF16)
    vt = v.reshape(nblk, et, d).transpose(0, 2, 1).astype(BF16)
    last = nblk - 1
    blk = lambda s, lag: jnp.clip(s - lag, 0, last)
    tok = pl.BlockSpec((PEER_HEADS, PEER_NKEYS, tm), lambda i, s: (0, 0, i))
    grp = pl.BlockSpec((PEER_HEADS, 1, 8, tm), lambda i, s: (0, blk(s, 1) * ipb // 8, 0, i))
    return pl.pallas_call(
        functools.partial(_peer_dense_kernel, et=et, tm=tm),
        grid=(t // tm, nblk + 2),
        in_specs=[pl.BlockSpec((d, tm), lambda i, s: (0, i)),
                  pl.BlockSpec((et, d), lambda i, s: (blk(s, 0), 0)),
                  pl.BlockSpec((1, d, et), lambda i, s: (blk(s, 2), 0, 0)),
                  tok, tok, grp, grp,
                  pl.BlockSpec((tm, d), lambda i, s: (i, 0))],
        out_specs=pl.BlockSpec((tm, d), lambda i, s: (i, 0)),
        out_shape=jax.ShapeDtypeStruct((t, d), F32),
        scratch_shapes=[pltpu.VMEM((d, tm), F32), pltpu.VMEM((et, tm), F32), pltpu.VMEM((et, tm), F32),
                        pltpu.VMEM((et, tm), BF16), pltpu.VMEM((et, tm), BF16)],
        compiler_params=_cparams("parallel", "arbitrary"),
        name="peer_dense",
    )(xnt, u, vt, rank2, e2, n_g, coef_g, h)


def peer_layer(h, norm_g, w_q, keys, u, v):
    q, xnt = matmul(h, w_q.astype(BF16), gain=norm_g, emit_xnt=True)
    rank2, e2, n_i, coef = peer_select(q, keys)
    return peer_dense(h, xnt, u, v, rank2, e2, n_i, coef)


def _rmsnorm_kernel(x_ref, g_ref, o_ref):
    x = x_ref[...]
    o_ref[...] = x * lax.rsqrt(jnp.mean(x * x, axis=-1, keepdims=True) + EPS) * g_ref[...]


def rmsnorm_final(x2d, g, tm=512):
    t, d = x2d.shape
    return pl.pallas_call(
        _rmsnorm_kernel,
        grid=(t // tm,),
        in_specs=[pl.BlockSpec((tm, d), lambda i: (i, 0)),
                  pl.BlockSpec((1, d), lambda i: (0, 0))],
        out_specs=pl.BlockSpec((tm, d), lambda i: (i, 0)),
        out_shape=jax.ShapeDtypeStruct((t, d), F32),
        compiler_params=_cparams("parallel"),
        name="rmsnorm_final",
    )(x2d, g.reshape(1, d).astype(F32))


def _cd_w_in(w):
    widths = (C_DINNER, C_CONV_CH, C_HEADS, D_HEADS * D_DH, D_LATENT, IDX_HEADS * IDX_DIM, IDX_DIM, IDX_HEADS)
    offs = np.concatenate([[0], np.cumsum(widths)])
    z, xbc, dt, q, ckv, qidx, kidx, widx = [w[:, offs[i]:offs[i + 1]] for i in range(8)]
    k = w.shape[0]
    pad = lambda n: jnp.zeros((k, n), w.dtype)
    out = jnp.concatenate([z, xbc, q, qidx, ckv, kidx, pad(LANES - IDX_DIM),
                           dt, widx, pad(LANES - C_HEADS - IDX_HEADS)], axis=1)
    assert out.shape[1] == CD_WIDTH
    return out.astype(BF16)


def kernel(x, rel_table, ab_w_in, ab_w_out, ab_lambda, ab_a_norm, ab_b_norm, cd_w_in, cd_w_out, cd_conv_w, cd_conv_b, cd_dt_bias, cd_a_log, cd_d_skip, cd_ssm_norm, cd_kv_norm, cd_w_uk, cd_w_uv, norm_mix, norm_ffn, peer_w_q, peer_keys, peer_u, peer_v, norm_final):
    bsz, s_len, d = x.shape
    h = x.reshape(bsz * s_len, d)
    for layer in range(DEPTH):
        i = layer // 2
        if layer % 2 == 0:
            proj = matmul(h, ab_w_in[i].astype(BF16), gain=norm_mix[layer])
            oa = diff_attention(proj, rel_table, ab_lambda[i], ab_a_norm[i], bsz, s_len, layer)
            ob = retention(proj, ab_b_norm[i], bsz, s_len)
            mixed = jnp.concatenate([oa, ob], axis=-1)
            h = matmul(mixed, ab_w_out[i].astype(BF16), residual=h)
        else:
            proj = matmul(h, _cd_w_in(cd_w_in[i]), gain=norm_mix[layer])
            xconv = conv_silu(proj, cd_conv_w[i], cd_conv_b[i], bsz, s_len)
            yc = ssd_mixer(proj, xconv, cd_dt_bias[i], cd_a_log[i], cd_d_skip[i], cd_ssm_norm[i],
                           bsz, s_len)
            yd = dsa_mixer(proj, cd_kv_norm[i], cd_w_uk[i], cd_w_uv[i], rel_table, bsz, s_len)
            mixed = jnp.concatenate([yc, yd], axis=-1)
            h = matmul(mixed, cd_w_out[i].astype(BF16), residual=h)
        h = peer_layer(h, norm_ffn[layer], peer_w_q[layer], peer_keys[layer], peer_u[layer],
                       peer_v[layer])
    return rmsnorm_final(h, norm_final).reshape(bsz, s_len, d)
```

```python
import functools
import math

import jax
import jax.numpy as jnp
import numpy as np
from jax import lax
from jax.experimental import pallas as pl
from jax.experimental.pallas import tpu as pltpu

D_MODEL = 2048
DEPTH = 2
EPS = 1e-6
Q_BLOCK = 128
REL_BUCKETS = 32
REL_MAX_DIST = 128
A_HEADS = 8
A_DH = 64
A_DV = 2 * A_DH
B_HEADS = 8
B_DK = 64
B_DV = 128
RET_CHUNK = 128
ROPE_BASE = 10000.0
C_DINNER = D_MODEL
C_HEADDIM = 64
C_HEADS = C_DINNER // C_HEADDIM
C_GROUPS = 4
C_HPG = C_HEADS // C_GROUPS
C_DSTATE = 128
C_CONV = 4
C_CONV_CH = C_DINNER + 2 * C_GROUPS * C_DSTATE
SSD_CHUNK = 128
D_HEADS = 8
D_DH = 128
D_LATENT = 256
IDX_HEADS = 16
IDX_DIM = 64
IDX_TOPK_MAX = 256
PEER_HEADS = 8
PEER_NKEYS = 128
PEER_DQ = 256
PEER_TOPK = 16

F32 = jnp.float32
BF16 = jnp.bfloat16
NEG = -1e30
LANES = 128
VMEM_LIMIT = 56 * 1024 * 1024
HIGHEST = lax.Precision.HIGHEST

CD_Z = 0
CD_XBC = CD_Z + C_DINNER
CD_Q = CD_XBC + C_CONV_CH
CD_QIDX = CD_Q + D_HEADS * D_DH
CD_CKV = CD_QIDX + IDX_HEADS * IDX_DIM
CD_KIDX = CD_CKV + D_LATENT
CD_MISC = CD_KIDX + LANES
CD_WIDTH = CD_MISC + LANES
MISC_WIDX = C_HEADS


def _dot(a, b):
    return jnp.dot(a, b, preferred_element_type=F32)


def _dot_nt(a, b):
    return lax.dot_general(a, b, (((1,), (1,)), ((), ())), preferred_element_type=F32)


def _cparams(*sem):
    return pltpu.CompilerParams(dimension_semantics=sem, vmem_limit_bytes=VMEM_LIMIT)


def rel_bucket(dist):
    n = jnp.maximum(dist, 0)
    max_exact = REL_BUCKETS // 2
    nf = jnp.maximum(n, 1).astype(F32)
    large = max_exact + (jnp.log(nf / max_exact) / math.log(REL_MAX_DIST / max_exact)
                         * (REL_BUCKETS - max_exact)).astype(jnp.int32)
    large = jnp.minimum(large, REL_BUCKETS - 1)
    return jnp.where(n < max_exact, n, large)


def _near_bias(rel_table, tq):
    assert tq >= REL_MAX_DIST
    nh = rel_table.shape[1]
    dist = jnp.arange(-(tq - 1), 2 * tq)
    by_dist = rel_table[rel_bucket(dist)].astype(F32).T
    span = 3 * tq

    def first_row(k):
        ahead = by_dist[:, k - tq + 1:k + 1][:, ::-1]
        behind = by_dist[:, k + 1:k + tq][:, ::-1]
        return jnp.concatenate([ahead, jnp.zeros((nh, span - 2 * tq + 1), F32), behind], axis=1)

    rows = jnp.stack([first_row(tq - 1), first_row(2 * tq - 1)], axis=1)[:, :, None, :]
    tables = pl.pallas_call(
        functools.partial(_toeplitz_kernel, tq=tq),
        grid=(nh,),
        in_specs=[pl.BlockSpec((1, 2, 1, span), lambda h: (h, 0, 0, 0))],
        out_specs=pl.BlockSpec((1, 2, tq, tq), lambda h: (h, 0, 0, 0)),
        out_shape=jax.ShapeDtypeStruct((nh, 2, tq, tq), F32),
        compiler_params=_cparams("parallel"),
        name="bias_tables",
    )(rows)
    r = jnp.arange(tq)[:, None]
    c = jnp.arange(tq)[None, :]
    return tables, (r - c) >= 0


def _toeplitz_kernel(row_ref, o_ref, *, tq):
    for j in range(2):
        x = jnp.broadcast_to(row_ref[0, j], (tq, row_ref.shape[-1]))
        o_ref[0, j] = pltpu.roll(x, 0, 1, stride=1, stride_axis=0)[:, :tq]


def _mm_kernel(*refs, norm, residual, emit_xnt):
    it = iter(refs)
    x_ref = next(it)
    g_ref = next(it) if norm else None
    w_ref = next(it)
    r_ref = next(it) if residual else None
    o_ref = next(it)
    xo_ref = next(it) if emit_xnt else None
    xn_ref = next(it)

    @pl.when(pl.program_id(1) == 0)
    def _():
        x = x_ref[...].astype(F32)
        if norm:
            x = x * lax.rsqrt(jnp.mean(x * x, axis=-1, keepdims=True) + EPS) * g_ref[...]
        xn_ref[...] = x.astype(BF16)
        if emit_xnt:
            xo_ref[...] = x.T.astype(BF16)

    acc = _dot(xn_ref[...], w_ref[...])
    if residual:
        acc = acc + r_ref[...]
    o_ref[...] = acc.astype(o_ref.dtype)


def matmul(x, w, *, gain=None, residual=None, emit_xnt=False, out_dtype=F32, tm=1024, tn=512):
    t, k = x.shape
    n = w.shape[1]
    tm = min(tm, t)
    assert t % tm == 0 and n % tn == 0 and w.shape[0] == k
    norm = gain is not None
    res = residual is not None
    in_specs = [pl.BlockSpec((tm, k), lambda i, j: (i, 0))]
    args = [x]
    if norm:
        in_specs.append(pl.BlockSpec((1, k), lambda i, j: (0, 0)))
        args.append(gain.reshape(1, k).astype(F32))
    in_specs.append(pl.BlockSpec((k, tn), lambda i, j: (0, j)))
    args.append(w)
    if res:
        in_specs.append(pl.BlockSpec((tm, tn), lambda i, j: (i, j)))
        args.append(residual)
    out_specs = [pl.BlockSpec((tm, tn), lambda i, j: (i, j))]
    out_shape = [jax.ShapeDtypeStruct((t, n), out_dtype)]
    if emit_xnt:
        out_specs.append(pl.BlockSpec((k, tm), lambda i, j: (0, i)))
        out_shape.append(jax.ShapeDtypeStruct((k, t), BF16))
    outs = pl.pallas_call(
        functools.partial(_mm_kernel, norm=norm, residual=res, emit_xnt=emit_xnt),
        grid=(t // tm, n // tn),
        in_specs=in_specs,
        out_specs=out_specs,
        out_shape=out_shape,
        scratch_shapes=[pltpu.VMEM((tm, k), BF16)],
        compiler_params=_cparams("parallel", "arbitrary"),
        name="matmul",
    )(*args)
    return outs if emit_xnt else outs[0]


def _diffattn_kernel(far_ref, q_ref, k_ref, v_ref, bias_ref, lam_ref, g_ref, o_ref,
                     m_sc, l_sc, acc_sc, *, tq, lam_init):
    h = pl.program_id(1)
    qi = pl.program_id(2)
    q = (q_ref[...] * (A_DH ** -0.5)).astype(BF16)
    qs = (q[:, :A_DH], q[:, A_DH:])
    m_sc[...] = jnp.full(m_sc.shape, NEG, F32)
    l_sc[...] = jnp.zeros(l_sc.shape, F32)
    acc_sc[...] = jnp.zeros(acc_sc.shape, F32)

    def process(j, bias):
        rows = pl.ds(pl.multiple_of(j * tq, tq), tq)
        kb = k_ref[rows, :].astype(BF16)
        vb = v_ref[rows, :].astype(BF16)
        for m in range(2):
            s = _dot_nt(qs[m], kb[:, m * A_DH:(m + 1) * A_DH]) + bias
            m_prev = m_sc[m]
            m_new = jnp.maximum(m_prev, jnp.max(s, axis=-1, keepdims=True))
            alpha = jnp.exp(m_prev - m_new)
            p = jnp.exp(s - m_new)
            l_sc[m] = alpha * l_sc[m] + jnp.sum(p, axis=-1, keepdims=True)
            acc_sc[m] = alpha * acc_sc[m] + _dot(p.astype(BF16), vb)
            m_sc[m] = m_new

    far = far_ref[h]

    def far_body(j, c):
        process(j, far)
        return c

    lax.fori_loop(0, jnp.maximum(qi - 1, 0), far_body, 0)

    @pl.when(qi > 0)
    def _():
        process(qi - 1, bias_ref[0, 1])

    process(qi, bias_ref[0, 0])

    lp = lam_ref[...]
    lam = (jnp.exp(jnp.sum(lp[0:1] * lp[1:2], keepdims=True))
           - jnp.exp(jnp.sum(lp[2:3] * lp[3:4], keepdims=True)) + lam_init)
    o = acc_sc[0] / l_sc[0] - lam * (acc_sc[1] / l_sc[1])
    o = o * lax.rsqrt(jnp.mean(o * o, axis=-1, keepdims=True) + EPS) * g_ref[...] * (1.0 - lam_init)
    o_ref[...] = o.astype(o_ref.dtype)


def diff_attention(proj, rel_table, lam_p, a_norm, bsz, s_len, layer, tq=512):
    t = bsz * s_len
    nq = s_len // tq
    lam_init = 0.8 - 0.6 * math.exp(-0.3 * layer)
    bias, causal = _near_bias(rel_table, tq)
    bias = bias.at[:, 0].set(jnp.where(causal[None], bias[:, 0], NEG))
    far = rel_table[REL_BUCKETS - 1].astype(F32)
    kcol = A_HEADS * 2 * A_DH // LANES
    return pl.pallas_call(
        functools.partial(_diffattn_kernel, tq=tq, lam_init=lam_init),
        grid=(bsz, A_HEADS, nq),
        in_specs=[
            pl.BlockSpec(memory_space=pltpu.SMEM),
            pl.BlockSpec((tq, LANES), lambda b, h, i: (b * nq + i, h)),
            pl.BlockSpec((s_len, LANES), lambda b, h, i: (b, kcol + h)),
            pl.BlockSpec((s_len, LANES), lambda b, h, i: (b, 2 * kcol + h)),
            pl.BlockSpec((1, 2, tq, tq), lambda b, h, i: (h, 0, 0, 0)),
            pl.BlockSpec((4, A_DH), lambda b, h, i: (0, 0)),
            pl.BlockSpec((1, A_DV), lambda b, h, i: (0, 0)),
        ],
        out_specs=pl.BlockSpec((tq, LANES), lambda b, h, i: (b * nq + i, h)),
        out_shape=jax.ShapeDtypeStruct((t, A_HEADS * A_DV), BF16),
        scratch_shapes=[pltpu.VMEM((2, tq, 1), F32), pltpu.VMEM((2, tq, 1), F32),
                        pltpu.VMEM((2, tq, A_DV), F32)],
        compiler_params=_cparams("parallel", "parallel", "arbitrary"),
        name="diff_attention",
    )(far, proj, proj, proj, bias, lam_p.astype(F32), a_norm.reshape(1, A_DV).astype(F32))


def _retention_kernel(q_ref, k_ref, v_ref, gate_ref, cos_ref, sin_ref, inner_ref, qdec_ref, kdec_ref,
                      cdec_ref, g_ref, o_ref, *, nchunks):
    c = RET_CHUNK
    lane = lax.broadcasted_iota(jnp.int32, (c, LANES), 1)
    even = (lane % 2) == 0

    def rope(x, cos, sin):
        partner = jnp.where(even, pltpu.roll(x, LANES - 1, axis=1), pltpu.roll(x, 1, axis=1))
        return x * cos + partner * sin

    def body(ci, states):
        rows = pl.ds(pl.multiple_of(ci * c, c), c)
        cos = cos_ref[rows, :]
        sin = sin_ref[rows, :]
        qr = rope(q_ref[rows, :], cos, sin) * (B_DK ** -0.5)
        kr = rope(k_ref[rows, :], cos, sin)
        qd = (qr * qdec_ref[0]).astype(BF16)
        kdt = (kr * kdec_ref[0]).T.astype(BF16)
        qb = qr.astype(BF16)
        kb = kr.astype(BF16)
        new_states = []
        for hh in range(2):
            sl = slice(hh * B_DK, (hh + 1) * B_DK)
            vs = slice(hh * B_DV, (hh + 1) * B_DV)
            vv = v_ref[rows, vs].astype(BF16)
            sc = _dot_nt(qb[:, sl], kb[:, sl]) * inner_ref[hh]
            o = _dot(sc.astype(BF16), vv) + _dot(qd[:, sl], states[hh].astype(BF16))
            new_states.append(states[hh] * cdec_ref[hh][0:1, :] + _dot(kdt[sl, :], vv))
            o = o * lax.rsqrt(jnp.mean(o * o, axis=-1, keepdims=True) + EPS) * g_ref[...]
            gt = gate_ref[rows, vs]
            o_ref[rows, vs] = (o * (gt * jax.nn.sigmoid(gt))).astype(o_ref.dtype)
        return tuple(new_states)

    zero = jnp.zeros((B_DK, B_DV), F32)
    lax.fori_loop(0, nchunks, body, (zero, zero))


def retention(proj, b_norm, bsz, s_len):
    t = bsz * s_len
    c = RET_CHUNK
    nh = B_HEADS
    log_gamma = jnp.log(1.0 - 2.0 ** (-5.0 - jnp.arange(nh, dtype=F32)))
    idx = jnp.arange(c, dtype=F32)
    rel = idx[:, None] - idx[None, :]
    inner = jnp.where(rel[None] >= 0, jnp.exp(rel[None] * log_gamma[:, None, None]), 0.0)
    q_decay = jnp.exp((idx[:, None] + 1.0) * log_gamma[None, :])
    k_decay = jnp.exp((c - 1.0 - idx[:, None]) * log_gamma[None, :])
    chunk_decay = jnp.exp(c * log_gamma)

    def pair_lanes(d):
        return jnp.repeat(d.T.reshape(nh // 2, 2, c).transpose(0, 2, 1), B_DK, axis=-1)

    cdec = jnp.broadcast_to(chunk_decay[:, None, None], (nh, 8, B_DV))
    inv = ROPE_BASE ** (-jnp.arange(0, B_DK, 2, dtype=F32) / B_DK)
    ang = jnp.arange(s_len, dtype=F32)[:, None] * inv[None, :]
    sign = jnp.tile(jnp.asarray([-1.0, 1.0], F32), B_DK // 2)
    cos = jnp.tile(jnp.repeat(jnp.cos(ang), 2, axis=-1), (1, 2))
    sin = jnp.tile(jnp.repeat(jnp.sin(ang), 2, axis=-1) * sign, (1, 2))
    base = (A_HEADS * 2 * A_DH * 2 + A_HEADS * A_DV) // LANES
    kblk = base + B_HEADS * B_DK // LANES
    vblk = (kblk + B_HEADS * B_DK // LANES) // 2
    gblk = vblk + B_HEADS * B_DV // (2 * LANES)
    return pl.pallas_call(
        functools.partial(_retention_kernel, nchunks=s_len // c),
        grid=(bsz, nh // 2),
        in_specs=[
            pl.BlockSpec((s_len, LANES), lambda b, p: (b, base + p)),
            pl.BlockSpec((s_len, LANES), lambda b, p: (b, kblk + p)),
            pl.BlockSpec((s_len, 2 * B_DV), lambda b, p: (b, vblk + p)),
            pl.BlockSpec((s_len, 2 * B_DV), lambda b, p: (b, gblk + p)),
            pl.BlockSpec((s_len, LANES), lambda b, p: (0, 0)),
            pl.BlockSpec((s_len, LANES), lambda b, p: (0, 0)),
            pl.BlockSpec((2, c, c), lambda b, p: (p, 0, 0)),
            pl.BlockSpec((1, c, LANES), lambda b, p: (p, 0, 0)),
            pl.BlockSpec((1, c, LANES), lambda b, p: (p, 0, 0)),
            pl.BlockSpec((2, 8, B_DV), lambda b, p: (p, 0, 0)),
            pl.BlockSpec((1, B_DV), lambda b, p: (0, 0)),
        ],
        out_specs=pl.BlockSpec((s_len, 2 * B_DV), lambda b, p: (b, p)),
        out_shape=jax.ShapeDtypeStruct((t, nh * B_DV), BF16),
        compiler_params=_cparams("parallel", "parallel"),
        name="retention",
    )(proj, proj, proj, proj, cos, sin, inner, pair_lanes(q_decay), pair_lanes(k_decay), cdec,
      b_norm.reshape(1, B_DV).astype(F32))


def _conv_kernel(x_ref, w_ref, b_ref, o_ref):
    x = x_ref[...]
    row = lax.broadcasted_iota(jnp.int32, x.shape, 0)
    acc = x * w_ref[C_CONV - 1:C_CONV, :] + b_ref[...]
    for j in range(1, C_CONV):
        xs = jnp.where(row >= j, pltpu.roll(x, j, axis=0), 0.0)
        acc = acc + xs * w_ref[C_CONV - 1 - j:C_CONV - j, :]
    o_ref[...] = acc * jax.nn.sigmoid(acc)


def conv_silu(proj, conv_w, conv_b, bsz, s_len, tc=512):
    t = bsz * s_len
    off = CD_XBC // tc
    return pl.pallas_call(
        _conv_kernel,
        grid=(bsz, C_CONV_CH // tc),
        in_specs=[pl.BlockSpec((s_len, tc), lambda b, j: (b, off + j)),
                  pl.BlockSpec((C_CONV, tc), lambda b, j: (0, j)),
                  pl.BlockSpec((1, tc), lambda b, j: (0, j))],
        out_specs=pl.BlockSpec((s_len, tc), lambda b, j: (b, j)),
        out_shape=jax.ShapeDtypeStruct((t, C_CONV_CH), F32),
        compiler_params=_cparams("parallel", "parallel"),
        name="conv_silu",
    )(proj, conv_w.astype(F32), conv_b.reshape(1, C_CONV_CH).astype(F32))


def _ssd_kernel(x_ref, b_ref, c_ref, z_ref, dt_ref, dtb_ref, alog_ref, dsk_ref, ng_ref, o_ref,
                st_sc, y_sc, *, nchunks):
    qn = SSD_CHUNK
    p = C_HEADDIM
    r = lax.broadcasted_iota(jnp.int32, (qn, qn), 0)
    cc = lax.broadcasted_iota(jnp.int32, (qn, qn), 1)
    causal = cc <= r
    t1 = jnp.where(causal, 1.0, 0.0)
    t2 = jnp.where(r > cc, 1.0, 0.0)
    a = -jnp.exp(alog_ref[0])
    dsk = dsk_ref[0]
    st_sc[...] = jnp.zeros(st_sc.shape, F32)

    def body(ci, carry):
        rows = pl.ds(pl.multiple_of(ci * qn, qn), qn)
        xc = x_ref[rows, :]
        bc = b_ref[rows, :]
        cm = c_ref[rows, :]
        dtr = dt_ref[0, 0, rows, :] + dtb_ref[0]
        dt = jnp.maximum(dtr, 0.0) + jnp.log(1.0 + jnp.exp(-jnp.abs(dtr)))
        dta = dt * a
        cmb = cm.astype(BF16)
        cb = _dot_nt(cmb, bc.astype(BF16))
        bt = bc.T.astype(BF16)
        heads = range(C_HPG)
        cols = [dta[:, h:h + 1] for h in heads]
        segs = [jnp.dot(t1, cols[h] * t2, precision=HIGHEST, preferred_element_type=F32) for h in heads]
        css = [segs[h][:, 0:1] + cols[h][0:1, :] for h in heads]
        xhs = [xc[:, h * p:(h + 1) * p] for h in heads]
        xdts = [xhs[h] * dt[:, h:h + 1] for h in heads]
        sts = [st_sc[h] for h in heads]
        mats = [(cb * jnp.where(causal, jnp.exp(segs[h]), 0.0)).astype(BF16) for h in heads]
        intra = [_dot(mats[h], xdts[h].astype(BF16)) for h in heads]
        inter = [_dot(cmb, sts[h].astype(BF16)) for h in heads]
        lasts = [css[h][qn - 1:qn, :] for h in heads]
        upd = [_dot(bt, (xdts[h] * jnp.exp(lasts[h] - css[h])).astype(BF16)) for h in heads]
        for h in heads:
            st_sc[h] = sts[h] * jnp.exp(lasts[h]) + upd[h]
            y_sc[:, h * p:(h + 1) * p] = (intra[h] + inter[h] * jnp.exp(css[h])
                                          + xhs[h] * dsk[:, h:h + 1])
        zz = z_ref[rows, :]
        y = y_sc[...] * (zz * jax.nn.sigmoid(zz))
        y = y * lax.rsqrt(jnp.mean(y * y, axis=-1, keepdims=True) + EPS) * ng_ref[...]
        o_ref[rows, :] = y.astype(o_ref.dtype)
        return carry

    lax.fori_loop(0, nchunks, body, 0)


def ssd_mixer(proj, xconv, dt_bias, a_log, d_skip, norm_g, bsz, s_len):
    t = bsz * s_len
    g = C_GROUPS
    gw = C_DINNER // g
    dtg = proj[:, CD_MISC:CD_MISC + C_HEADS].reshape(bsz, s_len, g, C_HPG).transpose(0, 2, 1, 3)
    per_group = lambda v: v.astype(F32).reshape(g, 1, C_HPG)
    nb = C_DINNER // LANES
    return pl.pallas_call(
        functools.partial(_ssd_kernel, nchunks=s_len // SSD_CHUNK),
        grid=(bsz, g),
        in_specs=[
            pl.BlockSpec((s_len, gw), lambda b, k: (b, k)),
            pl.BlockSpec((s_len, C_DSTATE), lambda b, k: (b, nb + k)),
            pl.BlockSpec((s_len, C_DSTATE), lambda b, k: (b, nb + g + k)),
            pl.BlockSpec((s_len, gw), lambda b, k: (b, k)),
            pl.BlockSpec((1, 1, s_len, C_HPG), lambda b, k: (b, k, 0, 0)),
            pl.BlockSpec((1, 1, C_HPG), lambda b, k: (k, 0, 0)),
            pl.BlockSpec((1, 1, C_HPG), lambda b, k: (k, 0, 0)),
            pl.BlockSpec((1, 1, C_HPG), lambda b, k: (k, 0, 0)),
            pl.BlockSpec((1, gw), lambda b, k: (0, k)),
        ],
        out_specs=pl.BlockSpec((s_len, gw), lambda b, k: (b, k)),
        out_shape=jax.ShapeDtypeStruct((t, C_DINNER), BF16),
        scratch_shapes=[pltpu.VMEM((C_HPG, C_DSTATE, C_HEADDIM), F32),
                        pltpu.VMEM((SSD_CHUNK, gw), F32)],
        compiler_params=_cparams("parallel", "parallel"),
        name="ssd",
    )(xconv, xconv, xconv, proj, dtg, per_group(dt_bias), per_group(a_log), per_group(d_skip),
      norm_g.reshape(1, C_DINNER).astype(F32))


def _dsa_kernel(far_ref, q_ref, qidx_ref, misc_ref, ckv_ref, kidx_ref, wuk_ref, wuv_ref, kvn_ref,
                band_ref, o_ref, ckvn_sc, kidx_sc, key_sc, lg_sc, lg2_sc, mask_sc, *, widths, topk):
    tq = Q_BLOCK
    qi = pl.program_id(1)

    @pl.when(qi == 0)
    def _():
        c = ckv_ref[...]
        ckvn_sc[...] = (c * lax.rsqrt(jnp.mean(c * c, axis=-1, keepdims=True) + EPS)
                        * kvn_ref[...]).astype(BF16)
        kidx_sc[...] = kidx_ref[:, :IDX_DIM].astype(BF16)

    near = jnp.where(qi > 0, 1.0, 0.0)
    ur = lax.broadcasted_iota(jnp.int32, (LANES, LANES), 0)
    uc = lax.broadcasted_iota(jnp.int32, (LANES, LANES), 1)
    upper = jnp.where(ur <= uc, 1.0, 0.0).astype(BF16)
    kf = float(topk)
    int_min = jnp.int32(-2 ** 31)
    d0 = pl.ds(pl.multiple_of(qi * tq, tq), tq)
    d1 = pl.ds(pl.multiple_of(jnp.maximum(qi - 1, 0) * tq, tq), tq)

    def body(wd):
        qidx = qidx_ref[...].astype(BF16)
        w = misc_ref[:, MISC_WIDX:MISC_WIDX + IDX_HEADS] * ((IDX_HEADS * IDX_DIM) ** -0.5)
        kx = kidx_sc[:wd, :]
        sc = jnp.zeros((tq, wd), F32)
        for hi in range(IDX_HEADS):
            rel = _dot_nt(qidx[:, hi * IDX_DIM:(hi + 1) * IDX_DIM], kx)
            sc = sc + jnp.maximum(rel, 0.0) * w[:, hi:hi + 1]
        col = lax.broadcasted_iota(jnp.int32, (tq, wd), 1)
        row = lax.broadcasted_iota(jnp.int32, (tq, wd), 0) + qi * tq
        causal = col <= row
        sc = jnp.where(causal, sc, -jnp.inf)
        bits = pltpu.bitcast(sc, jnp.int32)
        key_sc[:, :wd] = jnp.where(bits < 0, bits ^ jnp.int32(0x7FFFFFFF), bits)

        def count_ge(cand):
            return jnp.sum(jnp.where(key_sc[:, :wd] >= cand, 1.0, 0.0), axis=-1, keepdims=True)

        zero = jnp.zeros((tq, 1), jnp.int32)
        prefix = jnp.where(count_ge(zero) >= kf, zero, zero + int_min)

        def bisect(i, prefix):
            cand = prefix | jnp.left_shift(jnp.int32(1), 30 - i)
            return jnp.where(count_ge(cand) >= kf, cand, prefix)

        thr = lax.fori_loop(0, 31, bisect, prefix)
        key = key_sc[:, :wd]
        gt = key > thr
        eq = key == thr
        need = kf - jnp.sum(jnp.where(gt, 1.0, 0.0), axis=-1, keepdims=True)
        eqf = jnp.where(eq, 1.0, 0.0)
        carry = jnp.zeros((tq, 1), F32)
        for j in range(wd // LANES):
            ls = slice(j * LANES, (j + 1) * LANES)
            e = eqf[:, ls]
            run = _dot(e.astype(BF16), upper) + carry
            carry = carry + jnp.sum(e, axis=-1, keepdims=True)
            take = jnp.where(gt[:, ls], 1.0, jnp.where(run <= need, e, 0.0))
            mask_sc[:, ls] = jnp.where(causal[:, ls], jnp.where(take > 0.5, 0.0, NEG), NEG)

        q = q_ref[...].astype(BF16)
        cw = 2 * LANES
        chunks = [slice(c0, min(c0 + cw, wd)) for c0 in range(0, wd, cw)]
        bufs = (lg_sc, lg2_sc)
        qa, top, den, ctx = {}, {}, {}, {}

        def logits_piece(h, ci):
            cs = chunks[ci]
            buf = bufs[h % 2]
            if ci == 0:
                qa[h] = _dot(q[:, h * D_DH:(h + 1) * D_DH], wuk_ref[h]).astype(BF16)
            lg = (_dot_nt(qa[h], ckvn_sc[cs, :]) * (D_DH ** -0.5)
                  + (mask_sc[:, cs] + far_ref[h]))
            buf[:, cs] = lg
            mx = jnp.max(lg, axis=-1, keepdims=True)
            top[h] = mx if ci == 0 else jnp.maximum(top[h], mx)
            if ci == len(chunks) - 1:
                buf[:, d0] += band_ref[h, 0]
                buf[:, d1] += band_ref[h, 1] * near
                top[h] = jnp.maximum(top[h], jnp.maximum(jnp.max(buf[:, d0], axis=-1, keepdims=True),
                                                         jnp.max(buf[:, d1], axis=-1, keepdims=True)))

        def context_piece(h, ci):
            cs = chunks[ci]
            pr = jnp.exp(bufs[h % 2][:, cs] - top[h])
            part = jnp.sum(pr, axis=-1, keepdims=True)
            den[h] = part if ci == 0 else den[h] + part
            upd = _dot(pr.astype(BF16), ckvn_sc[cs, :])
            ctx[h] = upd if ci == 0 else ctx[h] + upd
            if ci == len(chunks) - 1:
                o_ref[:, h * D_DH:(h + 1) * D_DH] = _dot((ctx[h] / den[h]).astype(BF16),
                                                         wuv_ref[h]).astype(o_ref.dtype)

        for h in range(D_HEADS + 1):
            for ci in range(len(chunks)):
                if h < D_HEADS:
                    logits_piece(h, ci)
                if h >= 1:
                    context_piece(h - 1, ci)

    hi_key = (qi + 1) * tq
    lo = 0
    for wd in widths:
        @pl.when((hi_key > lo) & (hi_key <= wd))
        def _():
            body(wd)
        lo = wd


def _key_widths(s_len, tq, levels=8):
    nq = s_len // tq
    return tuple(sorted({-(-nq * (k + 1) // levels) * tq for k in range(levels)}))


def dsa_mixer(proj, kv_norm, w_uk, w_uv, rel_table, bsz, s_len):
    t = bsz * s_len
    tq = Q_BLOCK
    nq = s_len // tq
    topk = min(IDX_TOPK_MAX, s_len // 4)
    bias, causal = _near_bias(rel_table, tq)
    far = rel_table[REL_BUCKETS - 1].astype(F32)
    band = bias - far[:, None, None, None]
    band = band.at[:, 0].set(jnp.where(causal[None], band[:, 0], 0.0))
    hw = D_HEADS * D_DH
    return pl.pallas_call(
        functools.partial(_dsa_kernel, widths=_key_widths(s_len, tq), topk=topk),
        grid=(bsz, nq),
        in_specs=[
            pl.BlockSpec(memory_space=pltpu.SMEM),
            pl.BlockSpec((tq, hw), lambda b, i: (b * nq + i, CD_Q // hw)),
            pl.BlockSpec((tq, hw), lambda b, i: (b * nq + i, CD_QIDX // hw)),
            pl.BlockSpec((tq, LANES), lambda b, i: (b * nq + i, CD_MISC // LANES)),
            pl.BlockSpec((s_len, D_LATENT), lambda b, i: (b, CD_CKV // D_LATENT)),
            pl.BlockSpec((s_len, LANES), lambda b, i: (b, CD_KIDX // LANES)),
            pl.BlockSpec((D_HEADS, D_DH, D_LATENT), lambda b, i: (0, 0, 0)),
            pl.BlockSpec((D_HEADS, D_LATENT, D_DH), lambda b, i: (0, 0, 0)),
            pl.BlockSpec((1, D_LATENT), lambda b, i: (0, 0)),
            pl.BlockSpec((D_HEADS, 2, tq, tq), lambda b, i: (0, 0, 0, 0)),
        ],
        out_specs=pl.BlockSpec((tq, hw), lambda b, i: (b * nq + i, 0)),
        out_shape=jax.ShapeDtypeStruct((t, hw), BF16),
        scratch_shapes=[pltpu.VMEM((s_len, D_LATENT), BF16), pltpu.VMEM((s_len, IDX_DIM), BF16),
                        pltpu.VMEM((tq, s_len), jnp.int32), pltpu.VMEM((tq, s_len), F32),
                        pltpu.VMEM((tq, s_len), F32), pltpu.VMEM((tq, s_len), F32)],
        compiler_params=_cparams("parallel", "arbitrary"),
        name="dsa",
    )(far, proj, proj, proj, proj, proj, w_uk.astype(BF16), w_uv.astype(BF16),
      kv_norm.reshape(1, D_LATENT).astype(F32), band)


def _peer_tables():
    pairs = [(k1, k2) for k1 in range(PEER_TOPK) for k2 in range(PEER_TOPK)
             if (k1 + 1) * (k2 + 1) <= PEER_TOPK]
    n = PEER_NKEYS
    r1 = np.zeros((n, n), np.float32)
    r2 = np.zeros((n, n), np.float32)
    pad = np.full((n, 1), NEG, np.float32)
    for r, (k1, k2) in enumerate(pairs):
        r1[r, k1] = 1.0
        r2[r, k2] = 1.0
        pad[r, 0] = 0.0
    return jnp.asarray(r1), jnp.asarray(r2), jnp.asarray(pad), jnp.asarray(r1.T, dtype=BF16)


def _peer_select_kernel(q_ref, keys_ref, r1_ref, r2_ref, pad_ref, grp_ref,
                        rank2_ref, e2_ref, n_ref, coef_ref):
    tm = q_ref.shape[0]
    nk = PEER_NKEYS
    ridx = lax.broadcasted_iota(jnp.int32, (nk, tm), 0).astype(F32)
    kidx = lax.broadcasted_iota(jnp.int32, (PEER_TOPK, tm), 0).astype(F32)

    def extract(chains, track, by_index):
        ss = list(chains)
        ranks = [jnp.full((nk, tm), 99.0, F32) for _ in ss]
        vals = [jnp.zeros((PEER_TOPK, tm), F32) for _ in ss]
        for k in range(PEER_TOPK):
            ms = [jnp.max(s, axis=0, keepdims=True) for s in ss]
            if by_index:
                firsts = [jnp.min(jnp.where(s == m, ridx, 1e9), axis=0, keepdims=True)
                          for s, m in zip(ss, ms)]
                hits = [ridx == f for f in firsts]
            else:
                hits = [s == m for s, m in zip(ss, ms)]
            ss = [jnp.where(hit, -jnp.inf, s) for s, hit in zip(ss, hits)]
            if track:
                ranks = [jnp.where(hit, float(k), r) for r, hit in zip(ranks, hits)]
                vals = [jnp.where(kidx == float(k), m, v) for v, m in zip(vals, ms)]
        return ss, ranks, vals

    half = PEER_DQ // 2
    nh = q_ref.shape[1] // PEER_DQ
    pad_rows = jnp.zeros((nk - PEER_TOPK, tm), F32)
    scores = []
    for hh in range(nh):
        q = q_ref[:, hh * PEER_DQ:(hh + 1) * PEER_DQ].astype(BF16)
        scores.append(_dot_nt(keys_ref[hh, 0], q[:, :half]))
        scores.append(_dot_nt(keys_ref[hh, 1], q[:, half:]))
    def run(by_index):
        _, ranks, tops = extract(scores, True, by_index)
        cand0s = []
        for hh in range(nh):
            a1p = jnp.concatenate([tops[2 * hh], pad_rows], axis=0)
            a2p = jnp.concatenate([tops[2 * hh + 1], pad_rows], axis=0)
            cand0s.append(jnp.dot(r1_ref[...], a1p, precision=HIGHEST, preferred_element_type=F32)
                          + jnp.dot(r2_ref[...], a2p, precision=HIGHEST, preferred_element_type=F32)
                          + pad_ref[...])
        cands, _, _ = extract(cand0s, False, by_index)
        ties = jnp.zeros((1, tm), F32)
        for hh in range(nh):
            s1, s2 = scores[2 * hh], scores[2 * hh + 1]
            rank1, rank2 = ranks[2 * hh], ranks[2 * hh + 1]
            a1, a2 = tops[2 * hh], tops[2 * hh + 1]
            cand0, cand = cand0s[hh], cands[hh]
            taken = jnp.where((cand == -jnp.inf) & (cand0 > 0.5 * NEG), 1.0, 0.0)
            cnt = _dot(grp_ref[...], taken.astype(BF16))
            top = a1[0:1, :] + a2[0:1, :]
            zsum = jnp.sum(taken * jnp.exp(cand0 - top), axis=0, keepdims=True)
            n_i = jnp.zeros((nk, tm), F32)
            for k in range(PEER_TOPK):
                n_i = jnp.where(rank1 == float(k), cnt[k:k + 1, :], n_i)
            rank2_ref[hh] = rank2.astype(rank2_ref.dtype)
            e2_ref[hh] = jnp.exp(s2 - a2[0:1, :]).astype(e2_ref.dtype)
            n_ref[hh] = n_i.reshape(n_ref.shape[1:])
            coef_ref[hh] = (jnp.exp(s1 - a1[0:1, :]) / zsum).reshape(coef_ref.shape[1:])
            if not by_index:
                for removed in (jnp.where(rank1 < 99.0, 1.0, 0.0), jnp.where(rank2 < 99.0, 1.0, 0.0),
                                taken):
                    n_removed = jnp.sum(removed, axis=0, keepdims=True)
                    ties = ties + jnp.where(n_removed != float(PEER_TOPK), 1.0, 0.0)
        return ties

    ties = run(False)

    @pl.when(jnp.max(ties) > 0.0)
    def _():
        run(True)


def peer_select(q, keys, tm=256, hpb=PEER_HEADS):
    t = q.shape[0]
    r1, r2, pad, grp = _peer_tables()
    nk = PEER_NKEYS
    full = lambda shape: pl.BlockSpec(shape, lambda i, h: (0,) * len(shape))
    out_spec = pl.BlockSpec((hpb, nk, tm), lambda i, h: (h, 0, i))
    grp_spec = pl.BlockSpec((hpb, nk // 8, 8, tm), lambda i, h: (h, 0, 0, i))
    sds = lambda dt: jax.ShapeDtypeStruct((PEER_HEADS, nk, t), dt)
    grp_sds = jax.ShapeDtypeStruct((PEER_HEADS, nk // 8, 8, t), F32)
    return pl.pallas_call(
        _peer_select_kernel,
        grid=(t // tm, PEER_HEADS // hpb),
        in_specs=[pl.BlockSpec((tm, hpb * PEER_DQ), lambda i, h: (i, h)),
                  pl.BlockSpec((hpb, 2, nk, PEER_DQ // 2), lambda i, h: (h, 0, 0, 0)),
                  full((nk, nk)), full((nk, nk)), full((nk, 1)), full((nk, nk))],
        out_specs=[out_spec, out_spec, grp_spec, grp_spec],
        out_shape=[sds(BF16), sds(BF16), grp_sds, grp_sds],
        compiler_params=_cparams("parallel", "parallel"),
        name="peer_select",
    )(q, keys.astype(BF16), r1, r2, pad, grp)


def _peer_dense_kernel(xnt_ref, u_ref, vt_ref, rank2_ref, e2_ref, n_ref, coef_ref, h_ref, o_ref,
                       acc_sc, y0_sc, y1_sc, z0_sc, z1_sc, *, et, tm):
    s = pl.program_id(1)
    nk = PEER_NKEYS
    nblk = pl.num_programs(1) - 2
    ipb = et // nk
    assert ipb in (4, 8)
    first_i = jnp.clip(s - 1, 0, nblk - 1) * ipb
    upper_half = (first_i % 8) != 0
    zero, half, one = (jnp.asarray(c, BF16) for c in (0.0, 0.5, 1.0))
    gelu_c, gelu_a = jnp.asarray(0.7978845608028654, BF16), jnp.asarray(0.044715, BF16)

    def block_rows(ref, h, ls):
        rows = ref[h, 0, :, ls]
        if ipb < 8:
            rows = jnp.where(upper_half, rows[ipb:], rows[:ipb])
        return rows.astype(BF16)

    @pl.when(s == 0)
    def _():
        acc_sc[...] = jnp.zeros(acc_sc.shape, F32)
        y1_sc[...] = jnp.zeros((et, tm), F32)
        z0_sc[...] = jnp.zeros((et, tm), BF16)

    def step(y_new, y_old, z_new, z_old):
        d = vt_ref.shape[1]
        piece = 2 * LANES

        def scores(p):
            cs = slice(p * piece, (p + 1) * piece)
            y_new[:, cs] = _dot(u_ref[...], xnt_ref[:, cs])

        per_token = {}

        def rows_of(ref, h, lc, ls):
            if (id(ref), h, lc) not in per_token:
                per_token[(id(ref), h, lc)] = block_rows(ref, h, ls)
            return per_token[(id(ref), h, lc)]

        def gated(r, lc):
            rs = slice(r * nk, (r + 1) * nk)
            ls = slice(lc * LANES, (lc + 1) * LANES)
            w = jnp.zeros((nk, LANES), BF16)
            for h in range(PEER_HEADS):
                nb = rows_of(n_ref, h, lc, ls)[r:r + 1]
                cb = rows_of(coef_ref, h, lc, ls)[r:r + 1]
                gate = jnp.minimum(jnp.maximum(nb - rank2_ref[h, :, ls], zero), one)
                w = w + gate * (cb * e2_ref[h, :, ls])
            yy = y_old[rs, ls].astype(BF16)
            act = half * yy * (one + jnp.tanh(gelu_c * (yy + gelu_a * (yy * yy * yy))))
            z_new[rs, ls] = act * w

        def project(p):
            pr, pc = divmod(p, tm // piece)
            rows = slice(pr * piece, (pr + 1) * piece)
            cs = slice(pc * piece, (pc + 1) * piece)
            acc_sc[rows, cs] += _dot(vt_ref[0, rows, :], z_old[:, cs])

        tiles = [(r, lc) for lc in range(tm // LANES) for r in range(ipb)]
        per_proj = len(tiles) // ((d // piece) * (tm // piece))
        per_score = len(tiles) // (tm // piece)
        assert per_proj >= 1 and per_score >= 1
        for k, (r, lc) in enumerate(tiles):
            if k % per_score == 0:
                scores(k // per_score)
            if k % per_proj == 0:
                project(k // per_proj)
            gated(r, lc)

    @pl.when(s % 2 == 0)
    def _():
        step(y0_sc, y1_sc, z1_sc, z0_sc)

    @pl.when(s % 2 == 1)
    def _():
        step(y1_sc, y0_sc, z0_sc, z1_sc)

    @pl.when(s == pl.num_programs(1) - 1)
    def _():
        o_ref[...] = h_ref[...] + acc_sc[...].T


def peer_dense(h, xnt, u, v, rank2, e2, n_g, coef_g, tm=512, et=512):
    t, d = h.shape
    tm = min(tm, t)
    nblk = u.shape[0] // et
    ipb = et // PEER_NKEYS
    u = u.astype(BF16)
    vt = v.reshape(nblk, et, d).transpose(0, 2, 1).astype(BF16)
    last = nblk - 1
    blk = lambda s, lag: jnp.clip(s - lag, 0, last)
    tok = pl.BlockSpec((PEER_HEADS, PEER_NKEYS, tm), lambda i, s: (0, 0, i))
    grp = pl.BlockSpec((PEER_HEADS, 1, 8, tm), lambda i, s: (0, blk(s, 1) * ipb // 8, 0, i))
    return pl.pallas_call(
        functools.partial(_peer_dense_kernel, et=et, tm=tm),
        grid=(t // tm, nblk + 2),
        in_specs=[pl.BlockSpec((d, tm), lambda i, s: (0, i)),
                  pl.BlockSpec((et, d), lambda i, s: (blk(s, 0), 0)),
                  pl.BlockSpec((1, d, et), lambda i, s: (blk(s, 2), 0, 0)),
                  tok, tok, grp, grp,
                  pl.BlockSpec((tm, d), lambda i, s: (i, 0))],
        out_specs=pl.BlockSpec((tm, d), lambda i, s: (i, 0)),
        out_shape=jax.ShapeDtypeStruct((t, d), F32),
        scratch_shapes=[pltpu.VMEM((d, tm), F32), pltpu.VMEM((et, tm), F32), pltpu.VMEM((et, tm), F32),
                        pltpu.VMEM((et, tm), BF16), pltpu.VMEM((et, tm), BF16)],
        compiler_params=_cparams("parallel", "arbitrary"),
        name="peer_dense",
    )(xnt, u, vt, rank2, e2, n_g, coef_g, h)


def peer_layer(h, norm_g, w_q, keys, u, v):
    q, xnt = matmul(h, w_q.astype(BF16), gain=norm_g, emit_xnt=True)
    rank2, e2, n_i, coef = peer_select(q, keys)
    return peer_dense(h, xnt, u, v, rank2, e2, n_i, coef)


def _rmsnorm_kernel(x_ref, g_ref, o_ref):
    x = x_ref[...]
    o_ref[...] = x * lax.rsqrt(jnp.mean(x * x, axis=-1, keepdims=True) + EPS) * g_ref[...]


def rmsnorm_final(x2d, g, tm=512):
    t, d = x2d.shape
    return pl.pallas_call(
        _rmsnorm_kernel,
        grid=(t // tm,),
        in_specs=[pl.BlockSpec((tm, d), lambda i: (i, 0)),
                  pl.BlockSpec((1, d), lambda i: (0, 0))],
        out_specs=pl.BlockSpec((tm, d), lambda i: (i, 0)),
        out_shape=jax.ShapeDtypeStruct((t, d), F32),
        compiler_params=_cparams("parallel"),
        name="rmsnorm_final",
    )(x2d, g.reshape(1, d).astype(F32))


def _cd_w_in(w):
    widths = (C_DINNER, C_CONV_CH, C_HEADS, D_HEADS * D_DH, D_LATENT, IDX_HEADS * IDX_DIM, IDX_DIM, IDX_HEADS)
    offs = np.concatenate([[0], np.cumsum(widths)])
    z, xbc, dt, q, ckv, qidx, kidx, widx = [w[:, offs[i]:offs[i + 1]] for i in range(8)]
    k = w.shape[0]
    pad = lambda n: jnp.zeros((k, n), w.dtype)
    out = jnp.concatenate([z, xbc, q, qidx, ckv, kidx, pad(LANES - IDX_DIM),
                           dt, widx, pad(LANES - C_HEADS - IDX_HEADS)], axis=1)
    assert out.shape[1] == CD_WIDTH
    return out.astype(BF16)


def kernel(x, rel_table, ab_w_in, ab_w_out, ab_lambda, ab_a_norm, ab_b_norm, cd_w_in, cd_w_out, cd_conv_w, cd_conv_b, cd_dt_bias, cd_a_log, cd_d_skip, cd_ssm_norm, cd_kv_norm, cd_w_uk, cd_w_uv, norm_mix, norm_ffn, peer_w_q, peer_keys, peer_u, peer_v, norm_final):
    bsz, s_len, d = x.shape
    h = x.reshape(bsz * s_len, d)
    for layer in range(DEPTH):
        i = layer // 2
        if layer % 2 == 0:
            proj = matmul(h, ab_w_in[i].astype(BF16), gain=norm_mix[layer])
            oa = diff_attention(proj, rel_table, ab_lambda[i], ab_a_norm[i], bsz, s_len, layer)
            ob = retention(proj, ab_b_norm[i], bsz, s_len)
            mixed = jnp.concatenate([oa, ob], axis=-1)
            h = matmul(mixed, ab_w_out[i].astype(BF16), residual=h)
        else:
            proj = matmul(h, _cd_w_in(cd_w_in[i]), gain=norm_mix[layer])
            xconv = conv_silu(proj, cd_conv_w[i], cd_conv_b[i], bsz, s_len)
            yc = ssd_mixer(proj, xconv, cd_dt_bias[i], cd_a_log[i], cd_d_skip[i], cd_ssm_norm[i],
                           bsz, s_len)
            yd = dsa_mixer(proj, cd_kv_norm[i], cd_w_uk[i], cd_w_uv[i], rel_table, bsz, s_len)
            mixed = jnp.concatenate([yc, yd], axis=-1)
            h = matmul(mixed, cd_w_out[i].astype(BF16), residual=h)
        h = peer_layer(h, norm_ffn[layer], peer_w_q[layer], peer_keys[layer], peer_u[layer],
                       peer_v[layer])
    return rmsnorm_final(h, norm_final).reshape(bsz, s_len, d)
```
